```python
import jax, jax.numpy as jnp
from jax import lax
import numpy as np

D_MODEL = 2048
BATCH = 4
SEQ = 2048
DEPTH = 4

N_HEADS = 16
HEAD_DIM = D_MODEL // N_HEADS
D_FF = 4 * D_MODEL
N_META = 16
BLOCK = 128
N_A_LAYERS = DEPTH // 2
N_B_LAYERS = DEPTH - N_A_LAYERS
RMS_EPS = 1e-6
FGATE_BIAS_LO = 1.0
FGATE_BIAS_HI = 6.0

kernel_name = "yoco_fox_stickbreak_hybrid"


def rms_norm(x, g):
    xf = x.astype(jnp.float32)
    y = xf * lax.rsqrt(jnp.mean(xf * xf, axis=-1, keepdims=True) + RMS_EPS)
    return (y * g.astype(jnp.float32)).astype(x.dtype)


def split_heads(t):
    b, l, _ = t.shape
    return t.reshape(b, l, N_HEADS, HEAD_DIM)


def query_blocks(total_len):
    bounds = [(0, N_META)]
    for start in range(N_META, total_len, BLOCK):
        bounds.append((start, min(start + BLOCK, total_len)))
    return bounds


def forgetting_attention(q, k, v, cum_logf):
    scale = HEAD_DIM ** -0.5
    outs = []
    for qs, qe in query_blocks(q.shape[1]):
        s = jnp.einsum('bqhd,bkhd->bhqk', q[:, qs:qe], k[:, :qe]).astype(jnp.float32) * scale
        s = s + cum_logf[:, :, qs:qe, None] - cum_logf[:, :, None, :qe]
        q_pos = jnp.arange(qs, qe)
        k_pos = jnp.arange(qe)
        mask = k_pos[None, :] <= q_pos[:, None]
        s = jnp.where(mask, s, -jnp.inf)
        p = jax.nn.softmax(s, axis=-1).astype(v.dtype)
        outs.append(jnp.einsum('bhqk,bkhd->bqhd', p, v[:, :qe]))
    return jnp.concatenate(outs, axis=1)


def stick_breaking_attention(q, k, v):
    scale = HEAD_DIM ** -0.5
    outs = []
    for qs, qe in query_blocks(q.shape[1]):
        z = jnp.einsum('bqhd,bkhd->bhqk', q[:, qs:qe], k[:, :qe]).astype(jnp.float32) * scale
        q_pos = jnp.arange(qs, qe)
        k_pos = jnp.arange(qe)
        mask = k_pos[None, :] < q_pos[:, None]
        log_beta = jax.nn.log_sigmoid(z)
        log_one_minus = jnp.where(mask, jax.nn.log_sigmoid(-z), 0.0)
        suffix = lax.cumsum(log_one_minus, axis=3, reverse=True) - log_one_minus
        a = jnp.where(mask, jnp.exp(log_beta + suffix), 0.0).astype(v.dtype)
        outs.append(jnp.einsum('bhqk,bkhd->bqhd', a, v[:, :qe]))
    return jnp.concatenate(outs, axis=1)


def squared_relu_mlp(h, w_up, w_down):
    u = h @ w_up
    return jnp.square(jax.nn.relu(u)) @ w_down


def setup_inputs(seed: int = 0) -> dict:
    key = jax.random.key(seed)
    ks = jax.random.split(key, 16)
    f32 = jnp.float32
    d_in_scale = D_MODEL ** -0.5
    x = jax.random.normal(ks[0], (BATCH, SEQ, D_MODEL), f32)
    meta_tokens = jax.random.normal(ks[1], (N_META, D_MODEL), f32)
    norm_attn = 1.0 + 0.02 * jax.random.normal(ks[2], (DEPTH, D_MODEL), f32)
    norm_mlp = 1.0 + 0.02 * jax.random.normal(ks[3], (DEPTH, D_MODEL), f32)
    w_up = jax.random.normal(ks[4], (DEPTH, D_MODEL, D_FF), f32) * d_in_scale
    w_down = jax.random.normal(ks[5], (DEPTH, D_FF, D_MODEL), f32) * (D_FF ** -0.5)
    fox_w_in = jax.random.normal(ks[6], (N_A_LAYERS, D_MODEL, 3 * D_MODEL + N_HEADS), f32) * d_in_scale
    fox_b_f = (jnp.linspace(FGATE_BIAS_LO, FGATE_BIAS_HI, N_HEADS, dtype=f32)[None, :]
               + 0.1 * jax.random.normal(ks[7], (N_A_LAYERS, N_HEADS), f32))
    fox_w_o = jax.random.normal(ks[8], (N_A_LAYERS, D_MODEL, D_MODEL), f32) * d_in_scale
    kv_norm = 1.0 + 0.02 * jax.random.normal(ks[9], (D_MODEL,), f32)
    w_kv = jax.random.normal(ks[10], (D_MODEL, 2 * D_MODEL), f32) * d_in_scale
    sb_w_q = jax.random.normal(ks[11], (N_B_LAYERS, D_MODEL, D_MODEL), f32) * d_in_scale
    sb_w_o = jax.random.normal(ks[12], (N_B_LAYERS, D_MODEL, D_MODEL), f32) * d_in_scale
    final_norm = 1.0 + 0.02 * jax.random.normal(ks[13], (D_MODEL,), f32)
    return {"x": x, "meta_tokens": meta_tokens, "norm_attn": norm_attn, "norm_mlp": norm_mlp,
            "w_up": w_up, "w_down": w_down, "fox_w_in": fox_w_in, "fox_b_f": fox_b_f,
            "fox_w_o": fox_w_o, "kv_norm": kv_norm, "w_kv": w_kv, "sb_w_q": sb_w_q,
            "sb_w_o": sb_w_o, "final_norm": final_norm}


def reference(x, meta_tokens, norm_attn, norm_mlp, w_up, w_down, fox_w_in, fox_b_f, fox_w_o,
              kv_norm, w_kv, sb_w_q, sb_w_o, final_norm):
    b = x.shape[0]
    meta = jnp.broadcast_to(meta_tokens[None].astype(x.dtype), (b, N_META, D_MODEL))
    h = jnp.concatenate([meta, x], axis=1)
    length = h.shape[1]
    shared_k = None
    shared_v = None
    for layer in range(DEPTH):
        a = rms_norm(h, norm_attn[layer])
        if layer < N_A_LAYERS:
            i = layer
            proj = a @ fox_w_in[i]
            q, k, v, f_logit = jnp.split(proj, [D_MODEL, 2 * D_MODEL, 3 * D_MODEL], axis=-1)
            logf = jax.nn.log_sigmoid(f_logit.astype(jnp.float32) + fox_b_f[i].astype(jnp.float32))
            cum_logf = jnp.cumsum(logf, axis=1).transpose(0, 2, 1)
            o = forgetting_attention(split_heads(q), split_heads(k), split_heads(v), cum_logf)
            h = h + o.reshape(b, length, D_MODEL) @ fox_w_o[i]
        else:
            if layer == N_A_LAYERS:
                c = rms_norm(h, kv_norm)
                k_s, v_s = jnp.split(c @ w_kv, 2, axis=-1)
                shared_k = split_heads(k_s)
                shared_v = split_heads(v_s)
            i = layer - N_A_LAYERS
            q = split_heads(a @ sb_w_q[i])
            o = stick_breaking_attention(q, shared_k, shared_v)
            h = h + o.reshape(b, length, D_MODEL) @ sb_w_o[i]
        h = h + squared_relu_mlp(rms_norm(h, norm_mlp[layer]), w_up[layer], w_down[layer])
    return rms_norm(h, final_norm)[:, N_META:]
```

```python
import functools

import jax
import jax.numpy as jnp
from jax import lax
from jax.experimental import pallas as pl
from jax.experimental.pallas import tpu as pltpu

D_MODEL = 2048
N_HEADS = 16
HEAD_DIM = D_MODEL // N_HEADS
D_FF = 4 * D_MODEL
N_META = 16
DEPTH = 4
N_A_LAYERS = DEPTH // 2
RMS_EPS = 1e-6
QK_SCALE = HEAD_DIM ** -0.5

LANES = 128
ROW_TILE = 688
NORM_ROWS = 16
ATT_BLOCK = 256
NEG_BIG = -1e30
VMEM_LIMIT = 56 * 1024 * 1024

F32 = jnp.float32
BF16 = jnp.bfloat16


def _rms_norm_rows(x_ref, g_ref, a_ref):
    n_rows = x_ref.shape[0]
    g = g_ref[...]

    def body(i, _):
        r0 = pl.multiple_of(i * NORM_ROWS, NORM_ROWS)
        x = x_ref[pl.ds(r0, NORM_ROWS), :]
        ms = jnp.mean(x * x, axis=-1, keepdims=True)
        a_ref[pl.ds(r0, NORM_ROWS), :] = (x * lax.rsqrt(ms + RMS_EPS) * g).astype(a_ref.dtype)
        return 0

    lax.fori_loop(0, n_rows // NORM_ROWS, body, 0)


def _norm_matmul_kernel(x_ref, g_ref, w_ref, o_ref, a_ref, *, n_scaled_tiles, scale):
    n = pl.program_id(1)

    @pl.when(n == 0)
    def _():
        _rms_norm_rows(x_ref, g_ref, a_ref)

    acc = jnp.dot(a_ref[...], w_ref[...], preferred_element_type=F32)
    if n_scaled_tiles:
        acc = acc * jnp.where(n < n_scaled_tiles, jnp.float32(scale), jnp.float32(1.0))
    o_ref[...] = acc.astype(o_ref.dtype)


def _norm_matmul(h, g, w, *, out_dtype, tn, n_scaled_cols=0, scale=1.0, name):
    t, d = h.shape
    n_out = w.shape[1]
    grid = (t // ROW_TILE, n_out // tn)
    return pl.pallas_call(
        functools.partial(_norm_matmul_kernel, n_scaled_tiles=n_scaled_cols // tn, scale=scale),
        grid=grid,
        in_specs=[
            pl.BlockSpec((ROW_TILE, d), lambda m, n: (m, 0)),
            pl.BlockSpec((1, d), lambda m, n: (0, 0)),
            pl.BlockSpec((d, tn), lambda m, n: (0, n)),
        ],
        out_specs=pl.BlockSpec((ROW_TILE, tn), lambda m, n: (m, n)),
        out_shape=jax.ShapeDtypeStruct((t, n_out), out_dtype),
        scratch_shapes=[pltpu.VMEM((ROW_TILE, d), BF16)],
        compiler_params=pltpu.CompilerParams(
            dimension_semantics=("arbitrary", "arbitrary"), vmem_limit_bytes=VMEM_LIMIT),
        name=name,
    )(h, g.reshape(1, d), w)


def _matmul_residual_kernel(o_ref, w_ref, h_ref, out_ref):
    out_ref[...] = h_ref[...] + jnp.dot(o_ref[...], w_ref[...], preferred_element_type=F32)


def _matmul_residual(o, w, h, *, tn, name):
    t, d = o.shape
    n_out = w.shape[1]
    grid = (n_out // tn, t // ROW_TILE)
    return pl.pallas_call(
        _matmul_residual_kernel,
        grid=grid,
        in_specs=[
            pl.BlockSpec((ROW_TILE, d), lambda n, m: (m, 0)),
            pl.BlockSpec((d, tn), lambda n, m: (0, n)),
            pl.BlockSpec((ROW_TILE, tn), lambda n, m: (m, n)),
        ],
        out_specs=pl.BlockSpec((ROW_TILE, tn), lambda n, m: (m, n)),
        out_shape=jax.ShapeDtypeStruct((t, n_out), F32),
        compiler_params=pltpu.CompilerParams(
            dimension_semantics=("arbitrary", "arbitrary"), vmem_limit_bytes=VMEM_LIMIT),
        name=name,
    )(o, w, h)


def _mlp_kernel(x_ref, g_ref, wu_ref, wd_ref, gf_ref, o_ref, a_ref, *, final_norm):
    f = pl.program_id(1)

    @pl.when(f == 0)
    def _():
        _rms_norm_rows(x_ref, g_ref, a_ref)
        o_ref[...] = x_ref[...]

    u = jnp.dot(a_ref[...], wu_ref[...], preferred_element_type=F32)
    u = jnp.square(jnp.maximum(u, 0.0)).astype(BF16)
    o_ref[...] += jnp.dot(u, wd_ref[...], preferred_element_type=F32)

    if final_norm:
        @pl.when(f == pl.num_programs(1) - 1)
        def _():
            gf = gf_ref[...]

            def body(i, _):
                r0 = pl.multiple_of(i * NORM_ROWS, NORM_ROWS)
                y = o_ref[pl.ds(r0, NORM_ROWS), :]
                ms = jnp.mean(y * y, axis=-1, keepdims=True)
                o_ref[pl.ds(r0, NORM_ROWS), :] = y * lax.rsqrt(ms + RMS_EPS) * gf
                return 0

            lax.fori_loop(0, o_ref.shape[0] // NORM_ROWS, body, 0)


def _mlp(h, g, w_up, w_down, g_final, *, tf, final_norm, name):
    t, d = h.shape
    d_ff = w_up.shape[1]
    grid = (t // ROW_TILE, d_ff // tf)
    return pl.pallas_call(
        functools.partial(_mlp_kernel, final_norm=final_norm),
        grid=grid,
        in_specs=[
            pl.BlockSpec((ROW_TILE, d), lambda m, f: (m, 0)),
            pl.BlockSpec((1, d), lambda m, f: (0, 0)),
            pl.BlockSpec((d, tf), lambda m, f: (0, f)),
            pl.BlockSpec((tf, d), lambda m, f: (f, 0)),
            pl.BlockSpec((1, d), lambda m, f: (0, 0)),
        ],
        out_specs=pl.BlockSpec((ROW_TILE, d), lambda m, f: (m, 0)),
        out_shape=jax.ShapeDtypeStruct((t, d), F32),
        scratch_shapes=[pltpu.VMEM((ROW_TILE, d), BF16)],
        compiler_params=pltpu.CompilerParams(
            dimension_semantics=("arbitrary", "arbitrary"), vmem_limit_bytes=VMEM_LIMIT),
        name=name,
    )(h, g.reshape(1, d), w_up, w_down, g_final.reshape(1, d))


def _split3_bf16(x):
    hi = x.astype(BF16)
    r = x - hi.astype(F32)
    mid = r.astype(BF16)
    lo = (r - mid.astype(F32)).astype(BF16)
    return hi, mid, lo


def _gate_cumsum_kernel(f_ref, b_ref, c_ref):
    length = f_ref.shape[1]
    b = b_ref[...]
    carry = jnp.zeros((1, LANES), F32)
    for r0 in range(0, length, ATT_BLOCK):
        rows = min(ATT_BLOCK, length - r0)
        x = f_ref[0, r0:r0 + rows, :] + b
        logf = jnp.minimum(x, 0.0) - jnp.log1p(jnp.exp(-jnp.abs(x)))
        ri = lax.broadcasted_iota(jnp.int32, (rows, rows), 0)
        ci = lax.broadcasted_iota(jnp.int32, (rows, rows), 1)
        tri = jnp.where(ci <= ri, 1.0, 0.0).astype(BF16)
        hi, mid, lo = _split3_bf16(logf)
        c = (jnp.dot(tri, hi, preferred_element_type=F32)
             + jnp.dot(tri, mid, preferred_element_type=F32)
             + jnp.dot(tri, lo, preferred_element_type=F32)) + carry
        c_ref[0, r0:r0 + rows, :] = c
        carry = c[rows - 1:rows, :]


def _gate_cumsum(f_logit, bias):
    b, length, lanes = f_logit.shape
    return pl.pallas_call(
        _gate_cumsum_kernel,
        grid=(b,),
        in_specs=[
            pl.BlockSpec((1, length, lanes), lambda i: (i, 0, 0)),
            pl.BlockSpec((1, lanes), lambda i: (0, 0)),
        ],
        out_specs=pl.BlockSpec((1, length, lanes), lambda i: (i, 0, 0)),
        out_shape=jax.ShapeDtypeStruct((b, length, lanes), F32),
        compiler_params=pltpu.CompilerParams(dimension_semantics=("arbitrary",)),
        name="gate_cumsum",
    )(f_logit, bias)


def _qk(q, k):
    return lax.dot_general(q, k, (((1,), (1,)), ((), ())), preferred_element_type=F32)


def _fox_kernel(q_ref, k_ref, v_ref, ccol_ref, crow_ref, o_ref):
    length = q_ref.shape[0]
    blk = ATT_BLOCK
    for qi, q0 in enumerate(range(0, length, blk)):
        tq = min(blk, length - q0)
        q = q_ref[q0:q0 + tq, :]
        ct = ccol_ref[0, 0, q0:q0 + tq, :]

        def online_step(carry, s, v):
            m, l, acc = carry
            m_new = jnp.maximum(m, jnp.max(s, axis=-1, keepdims=True))
            alpha = jnp.exp(m - m_new)
            p = jnp.exp(s - m_new)
            l = alpha * l + jnp.sum(p, axis=-1, keepdims=True)
            acc = alpha * acc + jnp.dot(p.astype(BF16), v, preferred_element_type=F32)
            return m_new, l, acc

        def body(j, carry, q=q, ct=ct):
            k0 = pl.multiple_of(j * blk, blk)
            s = _qk(q, k_ref[pl.ds(k0, blk), :])
            s = s + (ct - crow_ref[0, 0, pl.ds(j, 1), :])
            return online_step(carry, s, v_ref[pl.ds(k0, blk), :])

        init = (jnp.full((tq, 1), NEG_BIG, F32), jnp.zeros((tq, 1), F32),
                jnp.zeros((tq, HEAD_DIM), F32))
        carry = lax.fori_loop(0, qi, body, init)

        s = _qk(q, k_ref[q0:q0 + tq, :])
        s = s + (ct - crow_ref[0, 0, qi:qi + 1, 0:tq])
        ri = lax.broadcasted_iota(jnp.int32, (tq, tq), 0)
        ci = lax.broadcasted_iota(jnp.int32, (tq, tq), 1)
        s = jnp.where(ci <= ri, s, NEG_BIG)
        _, l, acc = online_step(carry, s, v_ref[q0:q0 + tq, :])
        o_ref[q0:q0 + tq, :] = (acc / l).astype(o_ref.dtype)


def _fox_attention(qkv, ccol, crow, batch):
    t = qkv.shape[0]
    length = t // batch
    n_blk = crow.shape[2]
    return pl.pallas_call(
        _fox_kernel,
        grid=(batch, N_HEADS),
        in_specs=[
            pl.BlockSpec((length, HEAD_DIM), lambda b, h: (b, h)),
            pl.BlockSpec((length, HEAD_DIM), lambda b, h: (b, N_HEADS + h)),
            pl.BlockSpec((length, HEAD_DIM), lambda b, h: (b, 2 * N_HEADS + h)),
            pl.BlockSpec((1, 1, length, 1), lambda b, h: (b, h, 0, 0)),
            pl.BlockSpec((1, 1, n_blk, ATT_BLOCK), lambda b, h: (b, h, 0, 0)),
        ],
        out_specs=pl.BlockSpec((length, HEAD_DIM), lambda b, h: (b, h)),
        out_shape=jax.ShapeDtypeStruct((t, D_MODEL), BF16),
        compiler_params=pltpu.CompilerParams(
            dimension_semantics=("arbitrary", "arbitrary"), vmem_limit_bytes=VMEM_LIMIT),
        name="fox_attention",
    )(qkv, qkv, qkv, ccol, crow)


def _sb_kernel(q_ref, k_ref, v_ref, o_ref):
    length = q_ref.shape[0]
    blk = ATT_BLOCK

    def strict_upper(n):
        ji = lax.broadcasted_iota(jnp.int32, (n, n), 0)
        si = lax.broadcasted_iota(jnp.int32, (n, n), 1)
        return jnp.where(ji > si, 1.0, 0.0).astype(BF16)

    def suffix_sum(l1m, mat):
        hi = l1m.astype(BF16)
        lo = (l1m - hi.astype(F32)).astype(BF16)
        return (jnp.dot(hi, mat, preferred_element_type=F32)
                + jnp.dot(lo, mat, preferred_element_type=F32))

    mat_full = strict_upper(blk)

    for qi, q0 in enumerate(range(0, length, blk)):
        tq = min(blk, length - q0)
        q = q_ref[q0:q0 + tq, :]

        z = _qk(q, k_ref[q0:q0 + tq, :])
        ri = lax.broadcasted_iota(jnp.int32, (tq, tq), 0)
        ci = lax.broadcasted_iota(jnp.int32, (tq, tq), 1)
        mask = ci < ri
        lp = jnp.log(1.0 + jnp.exp(-jnp.abs(z)))
        l1m = jnp.where(mask, jnp.minimum(-z, 0.0) - lp, 0.0)
        logb = jnp.minimum(z, 0.0) - lp
        mat = mat_full if tq == blk else strict_upper(tq)
        a = jnp.where(mask, jnp.exp(logb + suffix_sum(l1m, mat)), 0.0)
        acc = jnp.dot(a.astype(BF16), v_ref[q0:q0 + tq, :], preferred_element_type=F32)
        run = jnp.sum(l1m, axis=-1, keepdims=True)

        def body(step, carry, q=q, qi=qi):
            run, acc = carry
            j = qi - 1 - step
            k0 = pl.multiple_of(j * blk, blk)
            z = _qk(q, k_ref[pl.ds(k0, blk), :])
            lp = jnp.log(1.0 + jnp.exp(-jnp.abs(z)))
            l1m = jnp.minimum(-z, 0.0) - lp
            logb = jnp.minimum(z, 0.0) - lp
            a = jnp.exp(logb + (suffix_sum(l1m, mat_full) + run))
            acc = acc + jnp.dot(a.astype(BF16), v_ref[pl.ds(k0, blk), :],
                                preferred_element_type=F32)
            run = run + jnp.sum(l1m, axis=-1, keepdims=True)
            return run, acc

        _, acc = lax.fori_loop(0, qi, body, (run, acc))
        o_ref[q0:q0 + tq, :] = acc.astype(o_ref.dtype)


def _sb_attention(q, kv, batch):
    t = q.shape[0]
    length = t // batch
    return pl.pallas_call(
        _sb_kernel,
        grid=(batch, N_HEADS),
        in_specs=[
            pl.BlockSpec((length, HEAD_DIM), lambda b, h: (b, h)),
            pl.BlockSpec((length, HEAD_DIM), lambda b, h: (b, h)),
            pl.BlockSpec((length, HEAD_DIM), lambda b, h: (b, N_HEADS + h)),
        ],
        out_specs=pl.BlockSpec((length, HEAD_DIM), lambda b, h: (b, h)),
        out_shape=jax.ShapeDtypeStruct((t, D_MODEL), BF16),
        compiler_params=pltpu.CompilerParams(
            dimension_semantics=("arbitrary", "arbitrary"), vmem_limit_bytes=VMEM_LIMIT),
        name="sb_attention",
    )(q, kv, kv)


def kernel(x, meta_tokens, norm_attn, norm_mlp, w_up, w_down, fox_w_in, fox_b_f, fox_w_o,
           kv_norm, w_kv, sb_w_q, sb_w_o, final_norm):
    batch, seq, d = x.shape
    length = N_META + seq
    t = batch * length
    assert d == D_MODEL and t % ROW_TILE == 0 and length % 16 == 0
    n_blk = pl.cdiv(length, ATT_BLOCK)

    meta = jnp.broadcast_to(meta_tokens[None].astype(x.dtype), (batch, N_META, d))
    h = jnp.concatenate([meta, x], axis=1).reshape(t, d)

    kv = None
    for layer in range(DEPTH):
        if layer < N_A_LAYERS:
            i = layer
            w_qkv = fox_w_in[i, :, :3 * d].astype(BF16)
            w_f = jnp.pad(fox_w_in[i, :, 3 * d:], ((0, 0), (0, LANES - N_HEADS))).astype(BF16)
            b_f = jnp.pad(fox_b_f[i].astype(F32), (0, LANES - N_HEADS)).reshape(1, LANES)
            qkv = _norm_matmul(h, norm_attn[layer], w_qkv, out_dtype=BF16, tn=1024,
                               n_scaled_cols=d, scale=QK_SCALE, name="fox_qkv_proj")
            f_logit = _norm_matmul(h, norm_attn[layer], w_f, out_dtype=F32, tn=LANES,
                                   name="fox_gate_proj")
            c = _gate_cumsum(f_logit.reshape(batch, length, LANES), b_f)
            c_heads = c[:, :, :N_HEADS].transpose(0, 2, 1)
            ccol = c_heads[..., None]
            crow = jnp.pad(c_heads, ((0, 0), (0, 0), (0, n_blk * ATT_BLOCK - length)))
            crow = crow.reshape(batch, N_HEADS, n_blk, ATT_BLOCK)
            o = _fox_attention(qkv, ccol, crow, batch)
            h = _matmul_residual(o, fox_w_o[i].astype(BF16), h, tn=1024, name="fox_out_proj")
        else:
            i = layer - N_A_LAYERS
            if kv is None:
                kv = _norm_matmul(h, kv_norm, w_kv.astype(BF16), out_dtype=BF16, tn=1024,
                                  name="shared_kv_proj")
            q = _norm_matmul(h, norm_attn[layer], sb_w_q[i].astype(BF16), out_dtype=BF16, tn=1024,
                             n_scaled_cols=d, scale=QK_SCALE, name="sb_q_proj")
            o = _sb_attention(q, kv, batch)
            h = _matmul_residual(o, sb_w_o[i].astype(BF16), h, tn=1024, name="sb_out_proj")
        h = _mlp(h, norm_mlp[layer], w_up[layer].astype(BF16), w_down[layer].astype(BF16),
                 final_norm, tf=512, final_norm=(layer == DEPTH - 1), name="mlp")
    return h.reshape(batch, length, d)[:, N_META:]
```

```python
import functools

import jax
import jax.numpy as jnp
from jax import lax
from jax.experimental import pallas as pl
from jax.experimental.pallas import tpu as pltpu

D_MODEL = 2048
N_HEADS = 16
HEAD_DIM = D_MODEL // N_HEADS
D_FF = 4 * D_MODEL
N_META = 16
DEPTH = 4
N_A_LAYERS = DEPTH // 2
RMS_EPS = 1e-6
LOG2E = 1.4426950408889634
QK_SCALE_LOG2 = HEAD_DIM ** -0.5 * LOG2E

LANES = 128
ROW_TILE = 688
NORM_ROWS = 16
NORM_UNROLL = 8
GATE_BLOCK = 256
SUB = 256
Q_TILE = 1024
ATT_LOOKAHEAD = 3
SB_TERMS_LAG = 3
SB_WEIGHTS_LAG = 3
NEG_BIG = -1e30
VMEM_LIMIT = 56 * 1024 * 1024

F32 = jnp.float32
BF16 = jnp.bfloat16


def _rms_norm_rows(x_ref, g_ref, a_ref):
    n_rows = x_ref.shape[0]
    g = g_ref[...]

    def body(i, _):
        r0 = pl.multiple_of(i * NORM_ROWS, NORM_ROWS)
        x = x_ref[pl.ds(r0, NORM_ROWS), :]
        ms = jnp.mean(x * x, axis=-1, keepdims=True)
        a_ref[pl.ds(r0, NORM_ROWS), :] = (x * lax.rsqrt(ms + RMS_EPS) * g).astype(a_ref.dtype)
        return 0

    lax.fori_loop(0, n_rows // NORM_ROWS, body, 0, unroll=NORM_UNROLL)


def _norm_matmul_kernel(x_ref, g_ref, w_ref, *rest, n_scaled_tiles, scale, with_side):
    if with_side:
        ws_ref, o_ref, side_ref, a_ref = rest
    else:
        o_ref, a_ref = rest
    n = pl.program_id(1)

    @pl.when(n == 0)
    def _():
        _rms_norm_rows(x_ref, g_ref, a_ref)
        if with_side:
            side_ref[...] = jnp.dot(a_ref[...], ws_ref[...], preferred_element_type=F32)

    acc = jnp.dot(a_ref[...], w_ref[...], preferred_element_type=F32)
    if n_scaled_tiles:
        acc = acc * jnp.where(n < n_scaled_tiles, jnp.float32(scale), jnp.float32(1.0))
    o_ref[...] = acc.astype(o_ref.dtype)


def _norm_matmul(h, g, w, *, n_out, out_dtype, tn, n_scaled_cols=0, scale=1.0, w_side=None,
                 w_layer=0, name):
    t, d = h.shape
    grid = (t // ROW_TILE, n_out // tn)
    with_side = w_side is not None
    if w.ndim == 3:
        w_spec = pl.BlockSpec((None, d, tn), lambda m, n: (w_layer, 0, n))
    else:
        w_spec = pl.BlockSpec((d, tn), lambda m, n: (0, n))
    in_specs = [
        pl.BlockSpec((ROW_TILE, d), lambda m, n: (m, 0)),
        pl.BlockSpec((1, d), lambda m, n: (0, 0)),
        w_spec,
    ]
    out_specs = pl.BlockSpec((ROW_TILE, tn), lambda m, n: (m, n))
    out_shape = jax.ShapeDtypeStruct((t, n_out), out_dtype)
    args = (h, g.reshape(1, d), w)
    if with_side:
        n_side = w_side.shape[1]
        in_specs.append(pl.BlockSpec((d, n_side), lambda m, n: (0, 0)))
        out_specs = (out_specs, pl.BlockSpec((ROW_TILE, n_side), lambda m, n: (m, 0)))
        out_shape = (out_shape, jax.ShapeDtypeStruct((t, n_side), F32))
        args += (w_side,)
    return pl.pallas_call(
        functools.partial(_norm_matmul_kernel, n_scaled_tiles=n_scaled_cols // tn, scale=scale,
                          with_side=with_side),
        grid=grid,
        in_specs=in_specs,
        out_specs=out_specs,
        out_shape=out_shape,
        scratch_shapes=[pltpu.VMEM((ROW_TILE, d), BF16)],
        compiler_params=pltpu.CompilerParams(
            dimension_semantics=("arbitrary", "arbitrary"), vmem_limit_bytes=VMEM_LIMIT),
        name=name,
    )(*args)


def _matmul_residual_kernel(o_ref, w_ref, h_ref, out_ref):
    out_ref[...] = h_ref[...] + jnp.dot(o_ref[...], w_ref[...], preferred_element_type=F32)


def _matmul_residual(o, w, layer, h, *, tn, name):
    t, d = o.shape
    n_out = w.shape[2]
    grid = (n_out // tn, t // ROW_TILE)
    return pl.pallas_call(
        _matmul_residual_kernel,
        grid=grid,
        in_specs=[
            pl.BlockSpec((ROW_TILE, d), lambda n, m: (m, 0)),
            pl.BlockSpec((None, d, tn), lambda n, m: (layer, 0, n)),
            pl.BlockSpec((ROW_TILE, tn), lambda n, m: (m, n)),
        ],
        out_specs=pl.BlockSpec((ROW_TILE, tn), lambda n, m: (m, n)),
        out_shape=jax.ShapeDtypeStruct((t, n_out), F32),
        compiler_params=pltpu.CompilerParams(
            dimension_semantics=("arbitrary", "arbitrary"), vmem_limit_bytes=VMEM_LIMIT),
        name=name,
    )(o, w, h)


def _mlp_kernel(x_ref, g_ref, wu_ref, wd_ref, gf_ref, o_ref, a_ref, *, final_norm):
    f = pl.program_id(1)

    @pl.when(f == 0)
    def _():
        _rms_norm_rows(x_ref, g_ref, a_ref)
        o_ref[...] = x_ref[...]

    u = jnp.dot(a_ref[...], wu_ref[...], preferred_element_type=F32)
    u = jnp.square(jnp.maximum(u, 0.0)).astype(BF16)
    o_ref[...] += jnp.dot(u, wd_ref[...], preferred_element_type=F32)

    if final_norm:
        @pl.when(f == pl.num_programs(1) - 1)
        def _():
            gf = gf_ref[...]

            def body(i, _):
                r0 = pl.multiple_of(i * NORM_ROWS, NORM_ROWS)
                y = o_ref[pl.ds(r0, NORM_ROWS), :]
                ms = jnp.mean(y * y, axis=-1, keepdims=True)
                o_ref[pl.ds(r0, NORM_ROWS), :] = y * lax.rsqrt(ms + RMS_EPS) * gf
                return 0

            lax.fori_loop(0, o_ref.shape[0] // NORM_ROWS, body, 0)


def _mlp(h, g, w_up, w_down, layer, g_final, *, tf, final_norm, name):
    t, d = h.shape
    d_ff = w_up.shape[2]
    grid = (t // ROW_TILE, d_ff // tf)
    return pl.pallas_call(
        functools.partial(_mlp_kernel, final_norm=final_norm),
        grid=grid,
        in_specs=[
            pl.BlockSpec((ROW_TILE, d), lambda m, f: (m, 0)),
            pl.BlockSpec((1, d), lambda m, f: (0, 0)),
            pl.BlockSpec((None, d, tf), lambda m, f: (layer, 0, f)),
            pl.BlockSpec((None, tf, d), lambda m, f: (layer, f, 0)),
            pl.BlockSpec((1, d), lambda m, f: (0, 0)),
        ],
        out_specs=pl.BlockSpec((ROW_TILE, d), lambda m, f: (m, 0)),
        out_shape=jax.ShapeDtypeStruct((t, d), F32),
        scratch_shapes=[pltpu.VMEM((ROW_TILE, d), BF16)],
        compiler_params=pltpu.CompilerParams(
            dimension_semantics=("arbitrary", "arbitrary"), vmem_limit_bytes=VMEM_LIMIT),
        name=name,
    )(h, g.reshape(1, d), w_up, w_down, g_final.reshape(1, d))


def _split3_bf16(x):
    hi = x.astype(BF16)
    r = x - hi.astype(F32)
    mid = r.astype(BF16)
    lo = (r - mid.astype(F32)).astype(BF16)
    return hi, mid, lo


def _gate_cumsum_kernel(f_ref, b_ref, c_ref):
    length = f_ref.shape[1]
    b = b_ref[...]
    carry = jnp.zeros((1, LANES), F32)
    for r0 in range(0, length, GATE_BLOCK):
        rows = min(GATE_BLOCK, length - r0)
        x = f_ref[0, r0:r0 + rows, :] + b
        logf = jnp.minimum(x, 0.0) - jnp.log1p(jnp.exp(-jnp.abs(x)))
        ri = lax.broadcasted_iota(jnp.int32, (rows, rows), 0)
        ci = lax.broadcasted_iota(jnp.int32, (rows, rows), 1)
        tri = jnp.where(ci <= ri, 1.0, 0.0).astype(BF16)
        hi, mid, lo = _split3_bf16(logf)
        c = (jnp.dot(tri, hi, preferred_element_type=F32)
             + jnp.dot(tri, mid, preferred_element_type=F32)
             + jnp.dot(tri, lo, preferred_element_type=F32)) + carry
        c_ref[0, r0:r0 + rows, :] = c * LOG2E
        carry = c[rows - 1:rows, :]


def _gate_cumsum(f_logit, bias):
    b, length, lanes = f_logit.shape
    return pl.pallas_call(
        _gate_cumsum_kernel,
        grid=(b,),
        in_specs=[
            pl.BlockSpec((1, length, lanes), lambda i: (i, 0, 0)),
            pl.BlockSpec((1, lanes), lambda i: (0, 0)),
        ],
        out_specs=pl.BlockSpec((1, length, lanes), lambda i: (i, 0, 0)),
        out_shape=jax.ShapeDtypeStruct((b, length, lanes), F32),
        compiler_params=pltpu.CompilerParams(dimension_semantics=("arbitrary",)),
        name="gate_cumsum",
    )(f_logit, bias)


def _nt(a, b):
    return lax.dot_general(a, b, (((1,), (1,)), ((), ())), preferred_element_type=F32)


def _pad_rows(x, rows):
    return jnp.concatenate([x, jnp.zeros((rows - x.shape[0], x.shape[1]), x.dtype)], axis=0)


def _key_le_query(n):
    ki = lax.broadcasted_iota(jnp.int32, (n, n), 0)
    qi = lax.broadcasted_iota(jnp.int32, (n, n), 1)
    return ki <= qi


def _key_lt_query(n):
    ki = lax.broadcasted_iota(jnp.int32, (n, n), 0)
    qi = lax.broadcasted_iota(jnp.int32, (n, n), 1)
    return ki < qi


def _fox_kernel(q_ref, k_ref, v_ref, ccol_ref, crow_ref, crowh_ref, o_ref,
                vt_ref, acc_ref, m_ref, l_ref):
    length = q_ref.shape[0]
    seq = length - N_META
    tq = min(Q_TILE, seq)
    n_sub = tq // SUB

    vt_ref[...] = v_ref[N_META:, :].T
    vt_head = v_ref[0:LANES, :].T
    k_meta = k_ref[0:N_META, :]
    cs_meta = ccol_ref[0, 0, 0:N_META, :]

    s = _nt(k_ref[0:LANES, :], q_ref[0:LANES, :])
    s = s + (crowh_ref[0, 0] - ccol_ref[0, 0, 0:LANES, :])
    s = jnp.where(_key_le_query(LANES), s, NEG_BIG)
    m = jnp.max(s, axis=0, keepdims=True)
    p = jnp.exp2(s - m)
    l = jnp.sum(p, axis=0, keepdims=True)
    acc = jnp.dot(vt_head, p.astype(BF16), preferred_element_type=F32)
    o_ref[0:N_META, :] = (acc * (1.0 / l)).T[0:N_META, :].astype(o_ref.dtype)

    for t0 in range(0, seq, tq):
        tasks = [(None, u, False) for u in range(n_sub)]
        tasks += [(j, u, False) for j in range(t0 // SUB) for u in range(n_sub)]
        tasks += [(t0 // SUB + jj, u, u == jj) for jj in range(n_sub) for u in range(jj, n_sub)]

        def q_sub(u, t0=t0):
            r = N_META + t0 + u * SUB
            return q_ref[r:r + SUB, :]

        def scores(task):
            j, u, _ = task
            k_blk = k_meta if j is None else k_ref[N_META + j * SUB:N_META + (j + 1) * SUB, :]
            return _nt(k_blk, q_sub(u))

        def consume(task, s, s_ahead, t0=t0):
            j, u, masked = task
            a0 = u * SUB
            ct = crow_ref[0, 0, :, t0 + a0:t0 + a0 + SUB]
            if j is None:
                s = s + (ct - cs_meta)
                m = jnp.max(s, axis=0, keepdims=True)
                p = jnp.exp2(s - m)
                m_ref[:, a0:a0 + SUB] = m
                l_ref[:, a0:a0 + SUB] = jnp.sum(p, axis=0, keepdims=True)
                acc_ref[:, a0:a0 + SUB] = jnp.dot(vt_head, _pad_rows(p.astype(BF16), LANES),
                                                  preferred_element_type=F32)
                return
            s = s + (ct - ccol_ref[0, 0, N_META + j * SUB:N_META + (j + 1) * SUB, :])
            if masked:
                s = jnp.where(_key_le_query(SUB), s, NEG_BIG)
            m_old = m_ref[:, a0:a0 + SUB]
            m_new = jnp.maximum(m_old, jnp.max(s, axis=0, keepdims=True))
            if s_ahead is not None:
                m_new = jnp.maximum(m_new, jnp.minimum(s_ahead[0:1, :], NEG_BIG))
            alpha = jnp.exp2(m_old - m_new)
            p = jnp.exp2(s - m_new)
            m_ref[:, a0:a0 + SUB] = m_new
            l_ref[:, a0:a0 + SUB] = (alpha * l_ref[:, a0:a0 + SUB]
                                     + jnp.sum(p, axis=0, keepdims=True))
            pv = jnp.dot(vt_ref[:, j * SUB:(j + 1) * SUB], p.astype(BF16),
                         preferred_element_type=F32)
            acc_ref[:, a0:a0 + SUB] = alpha * acc_ref[:, a0:a0 + SUB] + pv

        pending = {}
        for i in range(len(tasks) + ATT_LOOKAHEAD):
            if i < len(tasks):
                pending[i] = scores(tasks[i])
            if i >= ATT_LOOKAHEAD:
                consume(tasks[i - ATT_LOOKAHEAD], pending.pop(i - ATT_LOOKAHEAD), pending.get(i - 1))

        for u in range(n_sub):
            a0 = u * SUB
            r = N_META + t0 + a0
            o = acc_ref[:, a0:a0 + SUB] * (1.0 / l_ref[:, a0:a0 + SUB])
            o_ref[r:r + SUB, :] = o.T.astype(o_ref.dtype)


def _fox_attention(qkv, ccol, crow, crow_head, batch):
    t = qkv.shape[0]
    n_heads = qkv.shape[1] // (3 * HEAD_DIM)
    length = t // batch
    seq = length - N_META
    tq = min(Q_TILE, seq)
    return pl.pallas_call(
        _fox_kernel,
        grid=(batch, n_heads),
        in_specs=[
            pl.BlockSpec((length, HEAD_DIM), lambda b, h: (b, h)),
            pl.BlockSpec((length, HEAD_DIM), lambda b, h: (b, n_heads + h)),
            pl.BlockSpec((length, HEAD_DIM), lambda b, h: (b, 2 * n_heads + h)),
            pl.BlockSpec((1, 1, length, 1), lambda b, h: (b, h, 0, 0)),
            pl.BlockSpec((1, 1, 1, seq), lambda b, h: (b, h, 0, 0)),
            pl.BlockSpec((1, 1, 1, LANES), lambda b, h: (b, h, 0, 0)),
        ],
        out_specs=pl.BlockSpec((length, HEAD_DIM), lambda b, h: (b, h)),
        out_shape=jax.ShapeDtypeStruct((t, n_heads * HEAD_DIM), BF16),
        scratch_shapes=[
            pltpu.VMEM((HEAD_DIM, seq), BF16),
            pltpu.VMEM((HEAD_DIM, tq), F32),
            pltpu.VMEM((1, tq), F32),
            pltpu.VMEM((1, tq), F32),
        ],
        compiler_params=pltpu.CompilerParams(
            dimension_semantics=("arbitrary", "arbitrary"), vmem_limit_bytes=VMEM_LIMIT),
        name="fox_attention",
    )(qkv, qkv, qkv, ccol, crow, crow_head)


def _suffix_matrix(rows, period, width):
    si = lax.broadcasted_iota(jnp.int32, (rows, width), 0)
    ji = lax.broadcasted_iota(jnp.int32, (rows, width), 1)
    jm = jnp.where(ji >= period, ji - period, ji)
    return jnp.where((jm > si) & (jm < rows), 1.0, 0.0).astype(BF16)


def _zero_row_after(x):
    return jnp.minimum(jnp.maximum(x[0:1, :], 0.0), 0.0)


def _sb_kernel(q_ref, k_ref, v_ref, o_ref, vt_ref, acc_ref, run_ref, smat_ref):
    length = q_ref.shape[0]
    seq = length - N_META
    tq = min(Q_TILE, seq)
    n_sub = tq // SUB

    vt_ref[...] = v_ref[N_META:, :].T
    vt_head = v_ref[0:LANES, :].T
    k_meta = k_ref[0:N_META, :]
    smat_ref[...] = _suffix_matrix(SUB, SUB, 2 * SUB)
    smat_meta = _suffix_matrix(N_META, LANES, 2 * LANES)

    def terms(z, mask, suffix_mat, pad_to=None, z_ahead=None):
        neg_abs = lax.bitcast_convert_type(
            lax.bitcast_convert_type(z, jnp.uint32) | jnp.uint32(0x80000000), F32)
        to_log2 = LOG2E if z_ahead is None else LOG2E + _zero_row_after(z_ahead)
        pos = jnp.maximum(z, 0.0) + jnp.log(1.0 + jnp.exp2(neg_abs)) * to_log2
        if mask is not None:
            pos = jnp.where(mask, pos, 0.0)
        hi = pos.astype(BF16)
        lo = (pos - hi.astype(F32)).astype(BF16)
        if pad_to is not None:
            hi, lo = _pad_rows(hi, pad_to), _pad_rows(lo, pad_to)
        suffix = jnp.dot(suffix_mat, jnp.concatenate([hi, lo], axis=0),
                         preferred_element_type=F32)
        return z - pos, suffix, jnp.sum(pos, axis=0, keepdims=True)

    z = _nt(k_ref[0:LANES, :], q_ref[0:LANES, :])
    mask = _key_lt_query(LANES)
    w, suffix, _ = terms(z, mask, _suffix_matrix(LANES, LANES, 2 * LANES))
    a = jnp.where(mask, jnp.exp2(w - suffix), 0.0)
    acc = jnp.dot(vt_head, a.astype(BF16), preferred_element_type=F32)
    o_ref[0:N_META, :] = acc.T[0:N_META, :].astype(o_ref.dtype)

    for t0 in range(0, seq, tq):
        nb = t0 // SUB
        tasks = [(nb + jj, u, u == jj) for jj in reversed(range(n_sub)) for u in range(jj, n_sub)]
        tasks += [(j, u, False) for j in reversed(range(nb)) for u in range(n_sub)]
        tasks += [(None, u, False) for u in range(n_sub)]

        def q_sub(u, t0=t0):
            r = N_META + t0 + u * SUB
            return q_ref[r:r + SUB, :]

        def stage_scores(task):
            j, u, _ = task
            k_blk = k_meta if j is None else k_ref[N_META + j * SUB:N_META + (j + 1) * SUB, :]
            return _nt(k_blk, q_sub(u))

        def stage_terms(task, z, z_ahead):
            j, _, first = task
            if j is None:
                return terms(z, None, smat_meta, pad_to=LANES, z_ahead=z_ahead)
            return terms(z, _key_lt_query(SUB) if first else None, smat_ref[...], z_ahead=z_ahead)

        def stage_weights(task, w, suffix, tot, suffix_ahead, t0=t0):
            j, u, first = task
            a0 = u * SUB
            e = w - suffix
            if not first:
                run = run_ref[:, a0:a0 + SUB]
                if suffix_ahead is not None:
                    run = run + _zero_row_after(suffix_ahead)
                e = e - run
            a = jnp.exp2(e)
            if first:
                a = jnp.where(_key_lt_query(SUB), a, 0.0)
            if j is None:
                r = N_META + t0 + a0
                o = acc_ref[:, a0:a0 + SUB] + jnp.dot(vt_head, _pad_rows(a.astype(BF16), LANES),
                                                      preferred_element_type=F32)
                o_ref[r:r + SUB, :] = o.T.astype(o_ref.dtype)
                return
            pv = jnp.dot(vt_ref[:, j * SUB:(j + 1) * SUB], a.astype(BF16),
                         preferred_element_type=F32)
            if first:
                acc_ref[:, a0:a0 + SUB] = pv
                run_ref[:, a0:a0 + SUB] = tot
            else:
                acc_ref[:, a0:a0 + SUB] += pv
                run_ref[:, a0:a0 + SUB] += tot

        n = len(tasks)
        zs, ts = {}, {}
        for i in range(n + SB_TERMS_LAG + SB_WEIGHTS_LAG):
            if i < n:
                zs[i] = stage_scores(tasks[i])
            b = i - SB_TERMS_LAG
            c = b - SB_WEIGHTS_LAG
            if 0 <= b < n:
                ts[b] = stage_terms(tasks[b], zs.pop(b), zs.get(i - 1))
            if 0 <= c:
                ahead = ts.get(b - 1)
                stage_weights(tasks[c], *ts.pop(c), None if ahead is None else ahead[1])


def _sb_attention(q, kv, batch):
    t = q.shape[0]
    n_heads = q.shape[1] // HEAD_DIM
    length = t // batch
    seq = length - N_META
    tq = min(Q_TILE, seq)
    return pl.pallas_call(
        _sb_kernel,
        grid=(batch, n_heads),
        in_specs=[
            pl.BlockSpec((length, HEAD_DIM), lambda b, h: (b, h)),
            pl.BlockSpec((length, HEAD_DIM), lambda b, h: (b, h)),
            pl.BlockSpec((length, HEAD_DIM), lambda b, h: (b, n_heads + h)),
        ],
        out_specs=pl.BlockSpec((length, HEAD_DIM), lambda b, h: (b, h)),
        out_shape=jax.ShapeDtypeStruct((t, n_heads * HEAD_DIM), BF16),
        scratch_shapes=[
            pltpu.VMEM((HEAD_DIM, seq), BF16),
            pltpu.VMEM((HEAD_DIM, tq), F32),
            pltpu.VMEM((1, tq), F32),
            pltpu.VMEM((SUB, 2 * SUB), BF16),
        ],
        compiler_params=pltpu.CompilerParams(
            dimension_semantics=("arbitrary", "arbitrary"), vmem_limit_bytes=VMEM_LIMIT),
        name="sb_attention",
    )(q, kv, kv)


def kernel(x, meta_tokens, norm_attn, norm_mlp, w_up, w_down, fox_w_in, fox_b_f, fox_w_o,
           kv_norm, w_kv, sb_w_q, sb_w_o, final_norm):
    batch, seq, d = x.shape
    length = N_META + seq
    t = batch * length
    assert d == D_MODEL and t % ROW_TILE == 0 and seq % SUB == 0

    meta = jnp.broadcast_to(meta_tokens[None].astype(x.dtype), (batch, N_META, d))
    h = jnp.concatenate([meta, x], axis=1).reshape(t, d)

    fox_w_in_bf = fox_w_in.astype(BF16)
    fox_w_o_bf, sb_w_q_bf, sb_w_o_bf = fox_w_o.astype(BF16), sb_w_q.astype(BF16), sb_w_o.astype(BF16)
    w_up_bf, w_down_bf = w_up.astype(BF16), w_down.astype(BF16)

    kv = None
    for layer in range(DEPTH):
        if layer < N_A_LAYERS:
            i = layer
            w_f = jnp.pad(fox_w_in[i, :, 3 * d:], ((0, 0), (0, LANES - N_HEADS))).astype(BF16)
            b_f = jnp.pad(fox_b_f[i].astype(F32), (0, LANES - N_HEADS)).reshape(1, LANES)
            qkv, f_logit = _norm_matmul(h, norm_attn[layer], fox_w_in_bf, w_layer=i, n_out=3 * d,
                                        out_dtype=BF16, tn=1024, n_scaled_cols=d,
                                        scale=QK_SCALE_LOG2, w_side=w_f, name="fox_qkv_proj")
            c = _gate_cumsum(f_logit.reshape(batch, length, LANES), b_f)
            c_heads = c[:, :, :N_HEADS].transpose(0, 2, 1)
            ccol = c_heads[..., None]
            crow = c_heads[:, :, None, N_META:]
            crow_head = c_heads[:, :, None, :LANES]
            o = _fox_attention(qkv, ccol, crow, crow_head, batch)
            h = _matmul_residual(o, fox_w_o_bf, i, h, tn=1024, name="fox_out_proj")
        else:
            i = layer - N_A_LAYERS
            if kv is None:
                kv = _norm_matmul(h, kv_norm, w_kv.astype(BF16), n_out=2 * d, out_dtype=BF16,
                                  tn=1024, name="shared_kv_proj")
            q = _norm_matmul(h, norm_attn[layer], sb_w_q_bf, w_layer=i, n_out=d, out_dtype=BF16,
                             tn=1024, n_scaled_cols=d, scale=QK_SCALE_LOG2, name="sb_q_proj")
            o = _sb_attention(q, kv, batch)
            h = _matmul_residual(o, sb_w_o_bf, i, h, tn=1024, name="sb_out_proj")
        h = _mlp(h, norm_mlp[layer], w_up_bf, w_down_bf, layer, final_norm, tf=1024,
                 final_norm=(layer == DEPTH - 1), name="mlp")
    return h.reshape(batch, length, d)[:, N_META:]
```

```python
import functools

import jax
import jax.numpy as jnp
import numpy as np
from jax import lax
from jax.experimental import pallas as pl
from jax.experimental.pallas import tpu as pltpu

D_MODEL = 2048
N_HEADS = 16
HEAD_DIM = D_MODEL // N_HEADS
D_FF = 4 * D_MODEL
N_META = 16
DEPTH = 4
N_A_LAYERS = DEPTH // 2
RMS_EPS = 1e-6
LOG2E = 1.4426950408889634
QK_SCALE_LOG2 = HEAD_DIM ** -0.5 * LOG2E

LANES = 128
ROW_TILE = 688
NORM_ROWS = 16
NORM_UNROLL = 8
GATE_BLOCK = 688
SUB = 256
Q_TILE = 1024
ATT_LOOKAHEAD = 4
SB_TERMS_LAG = 3
SB_WEIGHTS_LAG = 3
NEG_BIG = -1e30
VMEM_LIMIT = 56 * 1024 * 1024

F32 = jnp.float32
BF16 = jnp.bfloat16


def _rms_norm_rows(x_ref, g_ref, a_ref):
    n_rows = x_ref.shape[0]
    g = g_ref[...]

    def body(i, _):
        r0 = pl.multiple_of(i * NORM_ROWS, NORM_ROWS)
        x = x_ref[pl.ds(r0, NORM_ROWS), :]
        ms = jnp.mean(x * x, axis=-1, keepdims=True)
        a_ref[pl.ds(r0, NORM_ROWS), :] = (x * lax.rsqrt(ms + RMS_EPS) * g).astype(a_ref.dtype)
        return 0

    lax.fori_loop(0, n_rows // NORM_ROWS, body, 0, unroll=NORM_UNROLL)


def _norm_matmul_kernel(x_ref, g_ref, w_ref, *rest, n_scaled_tiles, scale, with_side):
    if with_side:
        ws_ref, o_ref, side_ref, a_ref = rest
    else:
        o_ref, a_ref = rest
    n = pl.program_id(1)

    @pl.when(n == 0)
    def _():
        _rms_norm_rows(x_ref, g_ref, a_ref)
        if with_side:
            side_ref[...] = jnp.dot(a_ref[...], ws_ref[...], preferred_element_type=F32)

    acc = jnp.dot(a_ref[...], w_ref[...], preferred_element_type=F32)
    if n_scaled_tiles:
        acc = acc * jnp.where(n < n_scaled_tiles, jnp.float32(scale), jnp.float32(1.0))
    o_ref[...] = acc.astype(o_ref.dtype)


def _norm_matmul(h, g, w, *, n_out, out_dtype, tn, n_scaled_cols=0, scale=1.0, w_side=None,
                 w_layer=0, name):
    t, d = h.shape
    grid = (t // ROW_TILE, n_out // tn)
    with_side = w_side is not None
    if w.ndim == 3:
        w_spec = pl.BlockSpec((None, d, tn), lambda m, n: (w_layer, 0, n))
    else:
        w_spec = pl.BlockSpec((d, tn), lambda m, n: (0, n))
    in_specs = [
        pl.BlockSpec((ROW_TILE, d), lambda m, n: (m, 0)),
        pl.BlockSpec((1, d), lambda m, n: (0, 0)),
        w_spec,
    ]
    out_specs = pl.BlockSpec((ROW_TILE, tn), lambda m, n: (m, n))
    out_shape = jax.ShapeDtypeStruct((t, n_out), out_dtype)
    args = (h, g.reshape(1, d), w)
    if with_side:
        n_side = w_side.shape[1]
        in_specs.append(pl.BlockSpec((d, n_side), lambda m, n: (0, 0)))
        out_specs = (out_specs, pl.BlockSpec((ROW_TILE, n_side), lambda m, n: (m, 0)))
        out_shape = (out_shape, jax.ShapeDtypeStruct((t, n_side), F32))
        args += (w_side,)
    return pl.pallas_call(
        functools.partial(_norm_matmul_kernel, n_scaled_tiles=n_scaled_cols // tn, scale=scale,
                          with_side=with_side),
        grid=grid,
        in_specs=in_specs,
        out_specs=out_specs,
        out_shape=out_shape,
        scratch_shapes=[pltpu.VMEM((ROW_TILE, d), BF16)],
        compiler_params=pltpu.CompilerParams(
            dimension_semantics=("arbitrary", "arbitrary"), vmem_limit_bytes=VMEM_LIMIT),
        name=name,
    )(*args)


def _matmul_residual_kernel(o_ref, w_ref, h_ref, out_ref):
    out_ref[...] = h_ref[...] + jnp.dot(o_ref[...], w_ref[...], preferred_element_type=F32)


def _matmul_residual(o, w, layer, h, *, tn, name):
    t, d = o.shape
    n_out = w.shape[2]
    grid = (n_out // tn, t // ROW_TILE)
    return pl.pallas_call(
        _matmul_residual_kernel,
        grid=grid,
        in_specs=[
            pl.BlockSpec((ROW_TILE, d), lambda n, m: (m, 0)),
            pl.BlockSpec((None, d, tn), lambda n, m: (layer, 0, n)),
            pl.BlockSpec((ROW_TILE, tn), lambda n, m: (m, n)),
        ],
        out_specs=pl.BlockSpec((ROW_TILE, tn), lambda n, m: (m, n)),
        out_shape=jax.ShapeDtypeStruct((t, n_out), F32),
        compiler_params=pltpu.CompilerParams(
            dimension_semantics=("arbitrary", "arbitrary"), vmem_limit_bytes=VMEM_LIMIT),
        name=name,
    )(o, w, h)


def _mlp_kernel(x_ref, g_ref, wu_ref, wd_ref, gf_ref, o_ref, a_ref, *, final_norm):
    f = pl.program_id(1)

    @pl.when(f == 0)
    def _():
        _rms_norm_rows(x_ref, g_ref, a_ref)
        o_ref[...] = x_ref[...]

    u = jnp.dot(a_ref[...], wu_ref[...], preferred_element_type=F32)
    u = jnp.square(jnp.maximum(u, 0.0)).astype(BF16)
    o_ref[...] += jnp.dot(u, wd_ref[...], preferred_element_type=F32)

    if final_norm:
        @pl.when(f == pl.num_programs(1) - 1)
        def _():
            gf = gf_ref[...]

            def body(i, _):
                r0 = pl.multiple_of(i * NORM_ROWS, NORM_ROWS)
                y = o_ref[pl.ds(r0, NORM_ROWS), :]
                ms = jnp.mean(y * y, axis=-1, keepdims=True)
                o_ref[pl.ds(r0, NORM_ROWS), :] = y * lax.rsqrt(ms + RMS_EPS) * gf
                return 0

            lax.fori_loop(0, o_ref.shape[0] // NORM_ROWS, body, 0, unroll=NORM_UNROLL)


def _mlp(h, g, w_up, w_down, layer, g_final, *, tf, final_norm, name):
    t, d = h.shape
    d_ff = w_up.shape[2]
    grid = (t // ROW_TILE, d_ff // tf)
    return pl.pallas_call(
        functools.partial(_mlp_kernel, final_norm=final_norm),
        grid=grid,
        in_specs=[
            pl.BlockSpec((ROW_TILE, d), lambda m, f: (m, 0)),
            pl.BlockSpec((1, d), lambda m, f: (0, 0)),
            pl.BlockSpec((None, d, tf), lambda m, f: (layer, 0, f)),
            pl.BlockSpec((None, tf, d), lambda m, f: (layer, f, 0)),
            pl.BlockSpec((1, d), lambda m, f: (0, 0)),
        ],
        out_specs=pl.BlockSpec((ROW_TILE, d), lambda m, f: (m, 0)),
        out_shape=jax.ShapeDtypeStruct((t, d), F32),
        scratch_shapes=[pltpu.VMEM((ROW_TILE, d), BF16)],
        compiler_params=pltpu.CompilerParams(
            dimension_semantics=("arbitrary", "arbitrary"), vmem_limit_bytes=VMEM_LIMIT),
        name=name,
    )(h, g.reshape(1, d), w_up, w_down, g_final.reshape(1, d))


def _split3_bf16(x):
    hi = x.astype(BF16)
    r = x - hi.astype(F32)
    mid = r.astype(BF16)
    lo = (r - mid.astype(F32)).astype(BF16)
    return hi, mid, lo


def _gate_selectors(n_heads):
    n_parts = 3
    width = n_heads * LANES
    sel = np.zeros((n_parts * LANES, 2 * width), np.float32)
    ones = np.zeros((1, 2 * width), np.float32)
    for h in range(n_heads):
        for p in range(n_parts):
            sel[p * LANES + h, h * LANES + p] = 1.0
            sel[p * LANES + h, width + h * LANES + n_parts + p] = -1.0
            ones[0, h * LANES + n_parts + p] = 1.0
            ones[0, width + h * LANES + p] = 1.0
    return jnp.asarray(sel, BF16), jnp.asarray(ones, F32)


def _gate_kernel(f_ref, b_ref, sel_ref, ones_ref, qx_ref, kx_ref, carry_ref):
    rows = f_ref.shape[1]
    width = qx_ref.shape[1]

    @pl.when(pl.program_id(1) == 0)
    def _():
        carry_ref[...] = jnp.zeros_like(carry_ref)

    x = f_ref[0] + b_ref[...]
    logf = jnp.minimum(x, 0.0) - jnp.log1p(jnp.exp(-jnp.abs(x)))
    ri = lax.broadcasted_iota(jnp.int32, (rows, rows), 0)
    ci = lax.broadcasted_iota(jnp.int32, (rows, rows), 1)
    tri = jnp.where(ci <= ri, 1.0, 0.0).astype(BF16)
    hi, mid, lo = _split3_bf16(logf)
    c = (jnp.dot(tri, hi, preferred_element_type=F32)
         + jnp.dot(tri, mid, preferred_element_type=F32)
         + jnp.dot(tri, lo, preferred_element_type=F32)) + carry_ref[...]
    carry_ref[...] = c[rows - 1:rows, :]
    parts = jnp.concatenate(_split3_bf16(c * LOG2E), axis=1)
    out = jnp.dot(parts, sel_ref[...], preferred_element_type=F32) + ones_ref[...]
    qx_ref[...] = out[:, :width].astype(qx_ref.dtype)
    kx_ref[...] = out[:, width:].astype(kx_ref.dtype)


def _gate_bias_columns(f_logit, bias, n_heads):
    b, length, lanes = f_logit.shape
    n_blk = length // GATE_BLOCK
    width = n_heads * LANES
    sel, ones = _gate_selectors(n_heads)
    out = jax.ShapeDtypeStruct((b * length, width), BF16)
    out_spec = pl.BlockSpec((GATE_BLOCK, width), lambda i, r: (i * n_blk + r, 0))
    return pl.pallas_call(
        _gate_kernel,
        grid=(b, n_blk),
        in_specs=[
            pl.BlockSpec((1, GATE_BLOCK, lanes), lambda i, r: (i, r, 0)),
            pl.BlockSpec((1, lanes), lambda i, r: (0, 0)),
            pl.BlockSpec(sel.shape, lambda i, r: (0, 0)),
            pl.BlockSpec(ones.shape, lambda i, r: (0, 0)),
        ],
        out_specs=(out_spec, out_spec),
        out_shape=(out, out),
        scratch_shapes=[pltpu.VMEM((1, lanes), F32)],
        compiler_params=pltpu.CompilerParams(
            dimension_semantics=("arbitrary", "arbitrary"), vmem_limit_bytes=VMEM_LIMIT),
        name="gate_cumsum",
    )(f_logit, bias, sel, ones)


def _nt(a, b):
    return lax.dot_general(a, b, (((1,), (1,)), ((), ())), preferred_element_type=F32)


def _pad_rows(x, rows):
    return jnp.concatenate([x, jnp.zeros((rows - x.shape[0], x.shape[1]), x.dtype)], axis=0)


def _key_le_query(n):
    ki = lax.broadcasted_iota(jnp.int32, (n, n), 0)
    qi = lax.broadcasted_iota(jnp.int32, (n, n), 1)
    return ki <= qi


def _key_lt_query(n):
    ki = lax.broadcasted_iota(jnp.int32, (n, n), 0)
    qi = lax.broadcasted_iota(jnp.int32, (n, n), 1)
    return ki < qi


def _fox_kernel(q_ref, k_ref, v_ref, qx_ref, kx_ref, o_ref, vt_ref, qt_ref, acc_ref, m_ref, l_ref):
    length = q_ref.shape[0]
    seq = length - N_META
    tq = min(Q_TILE, seq)
    n_sub = tq // SUB

    def q_rows(r0, rows):
        return jnp.concatenate([q_ref[r0:r0 + rows, :], qx_ref[r0:r0 + rows, :]], axis=1)

    def k_rows(r0, rows):
        return jnp.concatenate([k_ref[r0:r0 + rows, :], kx_ref[r0:r0 + rows, :]], axis=1)

    vt_ref[...] = v_ref[N_META:, :].T
    vt_head = v_ref[0:LANES, :].T
    qt_ref[0:HEAD_DIM, :] = q_ref[N_META:, :].T
    qt_ref[HEAD_DIM:, :] = qx_ref[N_META:, :].T

    s = _nt(k_rows(0, LANES), q_rows(0, LANES))
    s = jnp.where(_key_le_query(LANES), s, NEG_BIG)
    m = jnp.max(s, axis=0, keepdims=True)
    p = jnp.exp2(s - m)
    l = jnp.sum(p, axis=0, keepdims=True)
    acc = jnp.dot(vt_head, p.astype(BF16), preferred_element_type=F32)
    o_ref[0:N_META, :] = (acc * (1.0 / l)).T[0:N_META, :].astype(o_ref.dtype)

    for t0 in range(0, seq, tq):
        tasks = [(None, u, False) for u in range(n_sub)]
        tasks += [(j, u, False) for j in range(t0 // SUB) for u in range(n_sub)]
        tasks += [(t0 // SUB + jj, u, u == jj) for jj in range(n_sub) for u in range(jj, n_sub)]

        def scores(task, t0=t0):
            j, u, _ = task
            k_blk = k_rows(0, N_META) if j is None else k_rows(N_META + j * SUB, SUB)
            c0 = t0 + u * SUB
            return jnp.dot(k_blk, qt_ref[:, c0:c0 + SUB], preferred_element_type=F32)

        def consume(task, s, s_ahead):
            j, u, masked = task
            a0 = u * SUB
            if j is None:
                m = jnp.max(s, axis=0, keepdims=True)
                p = jnp.exp2(s - m)
                m_ref[:, a0:a0 + SUB] = m
                l_ref[:, a0:a0 + SUB] = jnp.sum(p, axis=0, keepdims=True)
                acc_ref[:, a0:a0 + SUB] = jnp.dot(vt_head, _pad_rows(p.astype(BF16), LANES),
                                                  preferred_element_type=F32)
                return
            if masked:
                s = jnp.where(_key_le_query(SUB), s, NEG_BIG)
            m_old = m_ref[:, a0:a0 + SUB]
            m_new = jnp.maximum(m_old, jnp.max(s, axis=0, keepdims=True))
            if s_ahead is not None:
                m_new = m_new + _zero_row_after(s_ahead)
            alpha = jnp.exp2(m_old - m_new)
            p = jnp.exp2(s - m_new)
            m_ref[:, a0:a0 + SUB] = m_new
            l_ref[:, a0:a0 + SUB] = (alpha * l_ref[:, a0:a0 + SUB]
                                     + jnp.sum(p, axis=0, keepdims=True))
            pv = jnp.dot(vt_ref[:, j * SUB:(j + 1) * SUB], p.astype(BF16),
                         preferred_element_type=F32)
            acc_ref[:, a0:a0 + SUB] = alpha * acc_ref[:, a0:a0 + SUB] + pv

        pending = {}
        for i in range(len(tasks) + ATT_LOOKAHEAD):
            if i < len(tasks):
                pending[i] = scores(tasks[i])
            if i >= ATT_LOOKAHEAD:
                consume(tasks[i - ATT_LOOKAHEAD], pending.pop(i - ATT_LOOKAHEAD),
                        pending.get(i - ATT_LOOKAHEAD + 1))

        for u in range(n_sub):
            a0 = u * SUB
            r = N_META + t0 + a0
            o = acc_ref[:, a0:a0 + SUB] * (1.0 / l_ref[:, a0:a0 + SUB])
            o_ref[r:r + SUB, :] = o.T.astype(o_ref.dtype)


def _fox_attention(qkv, qx, kx, batch):
    t = qkv.shape[0]
    n_heads = qkv.shape[1] // (3 * HEAD_DIM)
    length = t // batch
    seq = length - N_META
    tq = min(Q_TILE, seq)
    return pl.pallas_call(
        _fox_kernel,
        grid=(batch, n_heads),
        in_specs=[
            pl.BlockSpec((length, HEAD_DIM), lambda b, h: (b, h)),
            pl.BlockSpec((length, HEAD_DIM), lambda b, h: (b, n_heads + h)),
            pl.BlockSpec((length, HEAD_DIM), lambda b, h: (b, 2 * n_heads + h)),
            pl.BlockSpec((length, LANES), lambda b, h: (b, h)),
            pl.BlockSpec((length, LANES), lambda b, h: (b, h)),
        ],
        out_specs=pl.BlockSpec((length, HEAD_DIM), lambda b, h: (b, h)),
        out_shape=jax.ShapeDtypeStruct((t, n_heads * HEAD_DIM), BF16),
        scratch_shapes=[
            pltpu.VMEM((HEAD_DIM, seq), BF16),
            pltpu.VMEM((HEAD_DIM + LANES, seq), BF16),
            pltpu.VMEM((HEAD_DIM, tq), F32),
            pltpu.VMEM((1, tq), F32),
            pltpu.VMEM((1, tq), F32),
        ],
        compiler_params=pltpu.CompilerParams(
            dimension_semantics=("arbitrary", "arbitrary"), vmem_limit_bytes=VMEM_LIMIT),
        name="fox_attention",
    )(qkv, qkv, qkv, qx, kx)


def _suffix_matrix(rows, period, width):
    si = lax.broadcasted_iota(jnp.int32, (rows, width), 0)
    ji = lax.broadcasted_iota(jnp.int32, (rows, width), 1)
    jm = jnp.where(ji >= period, ji - period, ji)
    return jnp.where((jm > si) & (jm < rows), 1.0, 0.0).astype(BF16)


def _zero_row_after(x):
    return jnp.minimum(jnp.maximum(x[0:1, :], 0.0), 0.0)


def _sb_kernel(q_ref, k_ref, v_ref, o_ref, vt_ref, qt_ref, acc_ref, run_ref, smat_ref):
    length = q_ref.shape[0]
    seq = length - N_META
    tq = min(Q_TILE, seq)
    n_sub = tq // SUB

    vt_ref[...] = v_ref[N_META:, :].T
    qt_ref[...] = q_ref[N_META:, :].T
    vt_head = v_ref[0:LANES, :].T
    k_meta = k_ref[0:N_META, :]
    smat_ref[...] = _suffix_matrix(SUB, SUB, 2 * SUB)
    smat_meta = _suffix_matrix(N_META, LANES, 2 * LANES)

    def terms(z, mask, suffix_mat, pad_to=None, z_ahead=None):
        one = 1.0 if z_ahead is None else 1.0 + _zero_row_after(z_ahead)
        pos = jnp.maximum(z, 0.0) + jnp.log2(one + jnp.exp2(-jnp.abs(z)))
        if mask is not None:
            pos = jnp.where(mask, pos, 0.0)
        hi = pos.astype(BF16)
        lo = (pos - hi.astype(F32)).astype(BF16)
        if pad_to is not None:
            hi, lo = _pad_rows(hi, pad_to), _pad_rows(lo, pad_to)
        suffix = jnp.dot(suffix_mat, jnp.concatenate([hi, lo], axis=0),
                         preferred_element_type=F32)
        return z - pos, suffix, jnp.sum(pos, axis=0, keepdims=True)

    z = _nt(k_ref[0:LANES, :], q_ref[0:LANES, :])
    mask = _key_lt_query(LANES)
    w, suffix, _ = terms(z, mask, _suffix_matrix(LANES, LANES, 2 * LANES))
    a = jnp.where(mask, jnp.exp2(w - suffix), 0.0)
    acc = jnp.dot(vt_head, a.astype(BF16), preferred_element_type=F32)
    o_ref[0:N_META, :] = acc.T[0:N_META, :].astype(o_ref.dtype)

    for t0 in range(0, seq, tq):
        nb = t0 // SUB
        tasks = [(nb + jj, u, u == jj) for jj in reversed(range(n_sub)) for u in range(jj, n_sub)]
        tasks += [(j, u, False) for j in reversed(range(nb)) for u in range(n_sub)]
        tasks += [(None, u, False) for u in range(n_sub)]

        def stage_scores(task, t0=t0):
            j, u, _ = task
            k_blk = k_meta if j is None else k_ref[N_META + j * SUB:N_META + (j + 1) * SUB, :]
            c0 = t0 + u * SUB
            return jnp.dot(k_blk, qt_ref[:, c0:c0 + SUB], preferred_element_type=F32)

        def stage_terms(task, z, z_ahead):
            j, _, first = task
            if j is None:
                return terms(z, None, smat_meta, pad_to=LANES, z_ahead=z_ahead)
            return terms(z, _key_lt_query(SUB) if first else None, smat_ref[...], z_ahead=z_ahead)

        def stage_weights(task, w, suffix, tot, suffix_ahead, t0=t0):
            j, u, first = task
            a0 = u * SUB
            e = w - suffix
            if not first:
                run = run_ref[:, a0:a0 + SUB]
                if suffix_ahead is not None:
                    run = run + _zero_row_after(suffix_ahead)
                e = e - run
            a = jnp.exp2(e)
            if first:
                a = jnp.where(_key_lt_query(SUB), a, 0.0)
            if j is None:
                r = N_META + t0 + a0
                o = acc_ref[:, a0:a0 + SUB] + jnp.dot(vt_head, _pad_rows(a.astype(BF16), LANES),
                                                      preferred_element_type=F32)
                o_ref[r:r + SUB, :] = o.T.astype(o_ref.dtype)
                return
            pv = jnp.dot(vt_ref[:, j * SUB:(j + 1) * SUB], a.astype(BF16),
                         preferred_element_type=F32)
            if first:
                acc_ref[:, a0:a0 + SUB] = pv
                run_ref[:, a0:a0 + SUB] = tot
            else:
                acc_ref[:, a0:a0 + SUB] += pv
                run_ref[:, a0:a0 + SUB] += tot

        n = len(tasks)
        zs, ts = {}, {}
        for i in range(n + SB_TERMS_LAG + SB_WEIGHTS_LAG):
            if i < n:
                zs[i] = stage_scores(tasks[i])
            b = i - SB_TERMS_LAG
            c = b - SB_WEIGHTS_LAG
            if 0 <= b < n:
                ts[b] = stage_terms(tasks[b], zs.pop(b), zs.get(b + 1))
            if 0 <= c:
                ahead = ts.get(c + 1)
                stage_weights(tasks[c], *ts.pop(c), None if ahead is None else ahead[1])


def _sb_attention(q, kv, batch):
    t = q.shape[0]
    n_heads = q.shape[1] // HEAD_DIM
    length = t // batch
    seq = length - N_META
    tq = min(Q_TILE, seq)
    return pl.pallas_call(
        _sb_kernel,
        grid=(batch, n_heads),
        in_specs=[
            pl.BlockSpec((length, HEAD_DIM), lambda b, h: (b, h)),
            pl.BlockSpec((length, HEAD_DIM), lambda b, h: (b, h)),
            pl.BlockSpec((length, HEAD_DIM), lambda b, h: (b, n_heads + h)),
        ],
        out_specs=pl.BlockSpec((length, HEAD_DIM), lambda b, h: (b, h)),
        out_shape=jax.ShapeDtypeStruct((t, n_heads * HEAD_DIM), BF16),
        scratch_shapes=[
            pltpu.VMEM((HEAD_DIM, seq), BF16),
            pltpu.VMEM((HEAD_DIM, seq), BF16),
            pltpu.VMEM((HEAD_DIM, tq), F32),
            pltpu.VMEM((1, tq), F32),
            pltpu.VMEM((SUB, 2 * SUB), BF16),
        ],
        compiler_params=pltpu.CompilerParams(
            dimension_semantics=("arbitrary", "arbitrary"), vmem_limit_bytes=VMEM_LIMIT),
        name="sb_attention",
    )(q, kv, kv)


def kernel(x, meta_tokens, norm_attn, norm_mlp, w_up, w_down, fox_w_in, fox_b_f, fox_w_o,
           kv_norm, w_kv, sb_w_q, sb_w_o, final_norm):
    batch, seq, d = x.shape
    length = N_META + seq
    t = batch * length
    assert d == D_MODEL and t % ROW_TILE == 0 and seq % SUB == 0

    meta = jnp.broadcast_to(meta_tokens[None].astype(x.dtype), (batch, N_META, d))
    h = jnp.concatenate([meta, x], axis=1).reshape(t, d)

    fox_w_qkv_bf = fox_w_in[:, :, :3 * d].astype(BF16)
    fox_w_o_bf, sb_w_q_bf, sb_w_o_bf = fox_w_o.astype(BF16), sb_w_q.astype(BF16), sb_w_o.astype(BF16)
    w_up_bf, w_down_bf = w_up.astype(BF16), w_down.astype(BF16)

    kv = None
    for layer in range(DEPTH):
        if layer < N_A_LAYERS:
            i = layer
            w_f = jnp.pad(fox_w_in[i, :, 3 * d:], ((0, 0), (0, LANES - N_HEADS))).astype(BF16)
            b_f = jnp.pad(fox_b_f[i].astype(F32), (0, LANES - N_HEADS)).reshape(1, LANES)
            qkv, f_logit = _norm_matmul(h, norm_attn[layer], fox_w_qkv_bf, w_layer=i, n_out=3 * d,
                                        out_dtype=BF16, tn=1024, n_scaled_cols=d,
                                        scale=QK_SCALE_LOG2, w_side=w_f, name="fox_qkv_proj")
            qx, kx = _gate_bias_columns(f_logit.reshape(batch, length, LANES), b_f, N_HEADS)
            o = _fox_attention(qkv, qx, kx, batch)
            h = _matmul_residual(o, fox_w_o_bf, i, h, tn=1024, name="fox_out_proj")
        else:
            i = layer - N_A_LAYERS
            if kv is None:
                kv = _norm_matmul(h, kv_norm, w_kv.astype(BF16), n_out=2 * d, out_dtype=BF16,
                                  tn=1024, name="shared_kv_proj")
            q = _norm_matmul(h, norm_attn[layer], sb_w_q_bf, w_layer=i, n_out=d, out_dtype=BF16,
                             tn=1024, n_scaled_cols=d, scale=QK_SCALE_LOG2, name="sb_q_proj")
            o = _sb_attention(q, kv, batch)
            h = _matmul_residual(o, sb_w_o_bf, i, h, tn=1024, name="sb_out_proj")
        h = _mlp(h, norm_mlp[layer], w_up_bf, w_down_bf, layer, final_norm, tf=1024,
                 final_norm=(layer == DEPTH - 1), name="mlp")
    return h.reshape(batch, length, d)[:, N_META:]
```

```python
import functools

import jax
import jax.numpy as jnp
import numpy as np
from jax import lax
from jax.experimental import pallas as pl
from jax.experimental.pallas import tpu as pltpu

D_MODEL = 2048
N_HEADS = 16
HEAD_DIM = D_MODEL // N_HEADS
D_FF = 4 * D_MODEL
N_META = 16
DEPTH = 4
N_A_LAYERS = DEPTH // 2
RMS_EPS = 1e-6
LOG2E = 1.4426950408889634
QK_SCALE_LOG2 = HEAD_DIM ** -0.5 * LOG2E

LANES = 128
ROW_TILE = 688
PROJ_TN = 2048
NORM_ROWS = 16
NORM_UNROLL = 8
GATE_BLOCK = 688
SUB = 256
Q_TILE = 1024
ATT_LOOKAHEAD = 4
SB_TERMS_LAG = 6
SB_WEIGHTS_LAG = 4
NEG_BIG = -1e30
VMEM_LIMIT = 56 * 1024 * 1024

F32 = jnp.float32
BF16 = jnp.bfloat16


def _rms_norm_rows(x_ref, g_ref, a_ref):
    n_rows = x_ref.shape[0]
    g = g_ref[...]

    def body(i, _):
        r0 = pl.multiple_of(i * NORM_ROWS, NORM_ROWS)
        x = x_ref[pl.ds(r0, NORM_ROWS), :]
        ms = jnp.mean(x * x, axis=-1, keepdims=True)
        a_ref[pl.ds(r0, NORM_ROWS), :] = (x * lax.rsqrt(ms + RMS_EPS) * g).astype(a_ref.dtype)
        return 0

    lax.fori_loop(0, n_rows // NORM_ROWS, body, 0, unroll=NORM_UNROLL)


def _cast_kernel(w_ref, o_ref):
    o_ref[...] = w_ref[...].astype(o_ref.dtype)


def _cast_leading_columns(w, n_cols, *, tn):
    layers, d, _ = w.shape
    return pl.pallas_call(
        _cast_kernel,
        grid=(layers, n_cols // tn),
        in_specs=[pl.BlockSpec((None, d, tn), lambda l, n: (l, 0, n))],
        out_specs=pl.BlockSpec((None, d, tn), lambda l, n: (l, 0, n)),
        out_shape=jax.ShapeDtypeStruct((layers, d, n_cols), BF16),
        compiler_params=pltpu.CompilerParams(
            dimension_semantics=("arbitrary", "arbitrary"), vmem_limit_bytes=VMEM_LIMIT),
        name="weight_cast",
    )(w)


def _norm_matmul_kernel(x_ref, g_ref, w_ref, *rest, n_scaled_tiles, scale, with_side):
    if with_side:
        ws_ref, o_ref, side_ref, a_ref = rest
    else:
        o_ref, a_ref = rest
    n = pl.program_id(1)

    @pl.when(n == 0)
    def _():
        _rms_norm_rows(x_ref, g_ref, a_ref)
        if with_side:
            side_ref[...] = jnp.dot(a_ref[...], ws_ref[...], preferred_element_type=F32)

    acc = jnp.dot(a_ref[...], w_ref[...], preferred_element_type=F32)
    if n_scaled_tiles:
        acc = acc * jnp.where(n < n_scaled_tiles, jnp.float32(scale), jnp.float32(1.0))
    o_ref[...] = acc.astype(o_ref.dtype)


def _norm_matmul(h, g, w, *, n_out, out_dtype, tn, n_scaled_cols=0, scale=1.0, w_side=None,
                 w_layer=0, name):
    t, d = h.shape
    grid = (t // ROW_TILE, n_out // tn)
    with_side = w_side is not None
    if w.ndim == 3:
        w_spec = pl.BlockSpec((None, d, tn), lambda m, n: (w_layer, 0, n))
    else:
        w_spec = pl.BlockSpec((d, tn), lambda m, n: (0, n))
    in_specs = [
        pl.BlockSpec((ROW_TILE, d), lambda m, n: (m, 0)),
        pl.BlockSpec((1, d), lambda m, n: (0, 0)),
        w_spec,
    ]
    out_specs = pl.BlockSpec((ROW_TILE, tn), lambda m, n: (m, n))
    out_shape = jax.ShapeDtypeStruct((t, n_out), out_dtype)
    args = (h, g.reshape(1, d), w)
    if with_side:
        n_side = w_side.shape[1]
        in_specs.append(pl.BlockSpec((d, n_side), lambda m, n: (0, 0)))
        out_specs = (out_specs, pl.BlockSpec((ROW_TILE, n_side), lambda m, n: (m, 0)))
        out_shape = (out_shape, jax.ShapeDtypeStruct((t, n_side), F32))
        args += (w_side,)
    return pl.pallas_call(
        functools.partial(_norm_matmul_kernel, n_scaled_tiles=n_scaled_cols // tn, scale=scale,
                          with_side=with_side),
        grid=grid,
        in_specs=in_specs,
        out_specs=out_specs,
        out_shape=out_shape,
        scratch_shapes=[pltpu.VMEM((ROW_TILE, d), BF16)],
        compiler_params=pltpu.CompilerParams(
            dimension_semantics=("arbitrary", "arbitrary"), vmem_limit_bytes=VMEM_LIMIT),
        name=name,
    )(*args)


def _matmul_residual_kernel(o_ref, w_ref, h_ref, out_ref):
    out_ref[...] = h_ref[...] + jnp.dot(o_ref[...], w_ref[...], preferred_element_type=F32)


def _matmul_residual(o, w, layer, h, *, tn, name):
    t, d = o.shape
    n_out = w.shape[2]
    grid = (n_out // tn, t // ROW_TILE)
    return pl.pallas_call(
        _matmul_residual_kernel,
        grid=grid,
        in_specs=[
            pl.BlockSpec((ROW_TILE, d), lambda n, m: (m, 0)),
            pl.BlockSpec((None, d, tn), lambda n, m: (layer, 0, n)),
            pl.BlockSpec((ROW_TILE, tn), lambda n, m: (m, n)),
        ],
        out_specs=pl.BlockSpec((ROW_TILE, tn), lambda n, m: (m, n)),
        out_shape=jax.ShapeDtypeStruct((t, n_out), F32),
        compiler_params=pltpu.CompilerParams(
            dimension_semantics=("arbitrary", "arbitrary"), vmem_limit_bytes=VMEM_LIMIT),
        name=name,
    )(o, w, h)


def _mlp_kernel(x_ref, g_ref, wu_ref, wd_ref, gf_ref, o_ref, a_ref, *, final_norm):
    f = pl.program_id(1)

    @pl.when(f == 0)
    def _():
        _rms_norm_rows(x_ref, g_ref, a_ref)
        o_ref[...] = x_ref[...]

    u = jnp.dot(a_ref[...], wu_ref[...], preferred_element_type=F32)
    u = jnp.square(jnp.maximum(u, 0.0)).astype(BF16)
    o_ref[...] += jnp.dot(u, wd_ref[...], preferred_element_type=F32)

    if final_norm:
        @pl.when(f == pl.num_programs(1) - 1)
        def _():
            gf = gf_ref[...]
            for r0 in range(0, o_ref.shape[0], NORM_ROWS):
                y = o_ref[r0:r0 + NORM_ROWS, :]
                ms = jnp.mean(y * y, axis=-1, keepdims=True)
                o_ref[r0:r0 + NORM_ROWS, :] = y * lax.rsqrt(ms + RMS_EPS) * gf


def _mlp(h, g, w_up, w_down, layer, g_final, *, tf, final_norm, name):
    t, d = h.shape
    d_ff = w_up.shape[2]
    grid = (t // ROW_TILE, d_ff // tf)
    return pl.pallas_call(
        functools.partial(_mlp_kernel, final_norm=final_norm),
        grid=grid,
        in_specs=[
            pl.BlockSpec((ROW_TILE, d), lambda m, f: (m, 0)),
            pl.BlockSpec((1, d), lambda m, f: (0, 0)),
            pl.BlockSpec((None, d, tf), lambda m, f: (layer, 0, f)),
            pl.BlockSpec((None, tf, d), lambda m, f: (layer, f, 0)),
            pl.BlockSpec((1, d), lambda m, f: (0, 0)),
        ],
        out_specs=pl.BlockSpec((ROW_TILE, d), lambda m, f: (m, 0)),
        out_shape=jax.ShapeDtypeStruct((t, d), F32),
        scratch_shapes=[pltpu.VMEM((ROW_TILE, d), BF16)],
        compiler_params=pltpu.CompilerParams(
            dimension_semantics=("arbitrary", "arbitrary"), vmem_limit_bytes=VMEM_LIMIT),
        name=name,
    )(h, g.reshape(1, d), w_up, w_down, g_final.reshape(1, d))


def _split3_bf16(x):
    hi = x.astype(BF16)
    r = x - hi.astype(F32)
    mid = r.astype(BF16)
    lo = (r - mid.astype(F32)).astype(BF16)
    return hi, mid, lo


def _gate_selectors(n_heads):
    n_parts = 3
    width = n_heads * LANES
    sel = np.zeros((n_parts * LANES, 2 * width), np.float32)
    ones = np.zeros((1, 2 * width), np.float32)
    for h in range(n_heads):
        for p in range(n_parts):
            sel[p * LANES + h, h * LANES + p] = 1.0
            sel[p * LANES + h, width + h * LANES + n_parts + p] = -1.0
            ones[0, h * LANES + n_parts + p] = 1.0
            ones[0, width + h * LANES + p] = 1.0
    return jnp.asarray(sel, BF16), jnp.asarray(ones, F32)


def _gate_kernel(f_ref, b_ref, sel_ref, ones_ref, qx_ref, kx_ref, carry_ref):
    rows = f_ref.shape[1]
    width = qx_ref.shape[1]

    @pl.when(pl.program_id(1) == 0)
    def _():
        carry_ref[...] = jnp.zeros_like(carry_ref)

    x = f_ref[0] + b_ref[...]
    logf = jnp.minimum(x, 0.0) - jnp.log1p(jnp.exp(-jnp.abs(x)))
    ri = lax.broadcasted_iota(jnp.int32, (rows, rows), 0)
    ci = lax.broadcasted_iota(jnp.int32, (rows, rows), 1)
    tri = jnp.where(ci <= ri, 1.0, 0.0).astype(BF16)
    hi, mid, lo = _split3_bf16(logf)
    c = (jnp.dot(tri, hi, preferred_element_type=F32)
         + jnp.dot(tri, mid, preferred_element_type=F32)
         + jnp.dot(tri, lo, preferred_element_type=F32)) + carry_ref[...]
    carry_ref[...] = c[rows - 1:rows, :]
    parts = jnp.concatenate(_split3_bf16(c * LOG2E), axis=1)
    out = jnp.dot(parts, sel_ref[...], preferred_element_type=F32) + ones_ref[...]
    qx_ref[...] = out[:, :width].astype(qx_ref.dtype)
    kx_ref[...] = out[:, width:].astype(kx_ref.dtype)


def _gate_bias_columns(f_logit, bias, n_heads):
    b, length, lanes = f_logit.shape
    n_blk = length // GATE_BLOCK
    width = n_heads * LANES
    sel, ones = _gate_selectors(n_heads)
    out = jax.ShapeDtypeStruct((b * length, width), BF16)
    out_spec = pl.BlockSpec((GATE_BLOCK, width), lambda i, r: (i * n_blk + r, 0))
    return pl.pallas_call(
        _gate_kernel,
        grid=(b, n_blk),
        in_specs=[
            pl.BlockSpec((1, GATE_BLOCK, lanes), lambda i, r: (i, r, 0)),
            pl.BlockSpec((1, lanes), lambda i, r: (0, 0)),
            pl.BlockSpec(sel.shape, lambda i, r: (0, 0)),
            pl.BlockSpec(ones.shape, lambda i, r: (0, 0)),
        ],
        out_specs=(out_spec, out_spec),
        out_shape=(out, out),
        scratch_shapes=[pltpu.VMEM((1, lanes), F32)],
        compiler_params=pltpu.CompilerParams(
            dimension_semantics=("arbitrary", "arbitrary"), vmem_limit_bytes=VMEM_LIMIT),
        name="gate_cumsum",
    )(f_logit, bias, sel, ones)


def _nt(a, b):
    return lax.dot_general(a, b, (((1,), (1,)), ((), ())), preferred_element_type=F32)


def _pad_rows(x, rows):
    return jnp.concatenate([x, jnp.zeros((rows - x.shape[0], x.shape[1]), x.dtype)], axis=0)


def _key_le_query(n):
    ki = lax.broadcasted_iota(jnp.int32, (n, n), 0)
    qi = lax.broadcasted_iota(jnp.int32, (n, n), 1)
    return ki <= qi


def _key_lt_query(n):
    ki = lax.broadcasted_iota(jnp.int32, (n, n), 0)
    qi = lax.broadcasted_iota(jnp.int32, (n, n), 1)
    return ki < qi


def _fox_kernel(q_ref, k_ref, v_ref, qx_ref, kx_ref, o_ref, vt_ref, qt_ref, acc_ref, m_ref, l_ref):
    length = q_ref.shape[0]
    seq = length - N_META
    tq = min(Q_TILE, seq)
    n_sub = tq // SUB

    def q_rows(r0, rows):
        return jnp.concatenate([q_ref[r0:r0 + rows, :], qx_ref[r0:r0 + rows, :]], axis=1)

    def k_rows(r0, rows):
        return jnp.concatenate([k_ref[r0:r0 + rows, :], kx_ref[r0:r0 + rows, :]], axis=1)

    vt_ref[...] = v_ref[N_META:, :].T
    vt_head = v_ref[0:LANES, :].T
    qt_ref[0:HEAD_DIM, :] = q_ref[N_META:, :].T
    qt_ref[HEAD_DIM:, :] = qx_ref[N_META:, :].T

    s = _nt(k_rows(0, LANES), q_rows(0, LANES))
    s = jnp.where(_key_le_query(LANES), s, NEG_BIG)
    m = jnp.max(s, axis=0, keepdims=True)
    p = jnp.exp2(s - m)
    l = jnp.sum(p, axis=0, keepdims=True)
    acc = jnp.dot(vt_head, p.astype(BF16), preferred_element_type=F32)
    o_ref[0:N_META, :] = (acc * (1.0 / l)).T[0:N_META, :].astype(o_ref.dtype)

    for t0 in range(0, seq, tq):
        tasks = [(None, u, False) for u in range(n_sub)]
        tasks += [(j, u, False) for j in range(t0 // SUB) for u in range(n_sub)]
        tasks += [(t0 // SUB + jj, u, u == jj) for jj in range(n_sub) for u in range(jj, n_sub)]

        def scores(task, t0=t0):
            j, u, _ = task
            k_blk = k_rows(0, N_META) if j is None else k_rows(N_META + j * SUB, SUB)
            c0 = t0 + u * SUB
            return jnp.dot(k_blk, qt_ref[:, c0:c0 + SUB], preferred_element_type=F32)

        def consume(task, s, s_ahead):
            j, u, masked = task
            a0 = u * SUB
            if j is None:
                m = jnp.max(s, axis=0, keepdims=True)
                p = jnp.exp2(s - m)
                m_ref[:, a0:a0 + SUB] = m
                l_ref[:, a0:a0 + SUB] = jnp.sum(p, axis=0, keepdims=True)
                acc_ref[:, a0:a0 + SUB] = jnp.dot(vt_head, _pad_rows(p.astype(BF16), LANES),
                                                  preferred_element_type=F32)
                return
            if masked:
                s = jnp.where(_key_le_query(SUB), s, NEG_BIG)
            m_old = m_ref[:, a0:a0 + SUB]
            m_new = jnp.maximum(m_old, jnp.max(s, axis=0, keepdims=True))
            if s_ahead is not None:
                m_new = m_new + _zero_row_after(s_ahead)
            alpha = jnp.exp2(m_old - m_new)
            p = jnp.exp2(s - m_new)
            m_ref[:, a0:a0 + SUB] = m_new
            l_ref[:, a0:a0 + SUB] = (alpha * l_ref[:, a0:a0 + SUB]
                                     + jnp.sum(p, axis=0, keepdims=True))
            pv = jnp.dot(vt_ref[:, j * SUB:(j + 1) * SUB], p.astype(BF16),
                         preferred_element_type=F32)
            acc_ref[:, a0:a0 + SUB] = alpha * acc_ref[:, a0:a0 + SUB] + pv

        pending = {}
        for i in range(len(tasks) + ATT_LOOKAHEAD):
            if i < len(tasks):
                pending[i] = scores(tasks[i])
            if i >= ATT_LOOKAHEAD:
                consume(tasks[i - ATT_LOOKAHEAD], pending.pop(i - ATT_LOOKAHEAD),
                        pending.get(i - ATT_LOOKAHEAD + 1))

        for u in range(n_sub):
            a0 = u * SUB
            r = N_META + t0 + a0
            o = acc_ref[:, a0:a0 + SUB] * (1.0 / l_ref[:, a0:a0 + SUB])
            o_ref[r:r + SUB, :] = o.T.astype(o_ref.dtype)


def _fox_attention(qkv, qx, kx, batch):
    t = qkv.shape[0]
    n_heads = qkv.shape[1] // (3 * HEAD_DIM)
    length = t // batch
    seq = length - N_META
    tq = min(Q_TILE, seq)
    return pl.pallas_call(
        _fox_kernel,
        grid=(batch, n_heads),
        in_specs=[
            pl.BlockSpec((length, HEAD_DIM), lambda b, h: (b, h)),
            pl.BlockSpec((length, HEAD_DIM), lambda b, h: (b, n_heads + h)),
            pl.BlockSpec((length, HEAD_DIM), lambda b, h: (b, 2 * n_heads + h)),
            pl.BlockSpec((length, LANES), lambda b, h: (b, h)),
            pl.BlockSpec((length, LANES), lambda b, h: (b, h)),
        ],
        out_specs=pl.BlockSpec((length, HEAD_DIM), lambda b, h: (b, h)),
        out_shape=jax.ShapeDtypeStruct((t, n_heads * HEAD_DIM), BF16),
        scratch_shapes=[
            pltpu.VMEM((HEAD_DIM, seq), BF16),
            pltpu.VMEM((HEAD_DIM + LANES, seq), BF16),
            pltpu.VMEM((HEAD_DIM, tq), F32),
            pltpu.VMEM((1, tq), F32),
            pltpu.VMEM((1, tq), F32),
        ],
        compiler_params=pltpu.CompilerParams(
            dimension_semantics=("arbitrary", "arbitrary"), vmem_limit_bytes=VMEM_LIMIT),
        name="fox_attention",
    )(qkv, qkv, qkv, qx, kx)


def _suffix_matrix(rows, width):
    si = lax.broadcasted_iota(jnp.int32, (rows, width), 0)
    ji = lax.broadcasted_iota(jnp.int32, (rows, width), 1)
    return jnp.where((ji > si) & (ji < rows), 1.0, 0.0).astype(BF16)


def _zero_row_after(x):
    return jnp.minimum(jnp.maximum(x[0:1, :], 0.0), 0.0)


def _sb_kernel(q_ref, k_ref, v_ref, o_ref, vt_ref, qt_ref, acc_ref, run_ref, smat_ref):
    length = q_ref.shape[0]
    seq = length - N_META
    tq = min(Q_TILE, seq)
    n_sub = tq // SUB

    vt_ref[...] = v_ref[N_META:, :].T
    qt_ref[...] = q_ref[N_META:, :].T
    vt_head = v_ref[0:LANES, :].T
    k_meta = k_ref[0:N_META, :]
    smat_ref[...] = _suffix_matrix(SUB, SUB)
    smat_meta = _suffix_matrix(N_META, LANES)

    def terms(z, mask, suffix_mat, pad_to=None, z_ahead=None):
        one = 1.0 if z_ahead is None else 1.0 + _zero_row_after(z_ahead)
        pos = jnp.maximum(z, 0.0) + jnp.log2(one + jnp.exp2(-jnp.abs(z)))
        if mask is not None:
            pos = jnp.where(mask, pos, 0.0)
        hi = pos.astype(BF16)
        if pad_to is not None:
            hi = _pad_rows(hi, pad_to)
        suffix = jnp.dot(suffix_mat, hi, preferred_element_type=F32)
        return z - pos, suffix, jnp.sum(pos, axis=0, keepdims=True)

    z = _nt(k_ref[0:LANES, :], q_ref[0:LANES, :])
    mask = _key_lt_query(LANES)
    w, suffix, _ = terms(z, mask, _suffix_matrix(LANES, LANES))
    a = jnp.where(mask, jnp.exp2(w - suffix), 0.0)
    acc = jnp.dot(vt_head, a.astype(BF16), preferred_element_type=F32)
    o_ref[0:N_META, :] = acc.T[0:N_META, :].astype(o_ref.dtype)

    for t0 in range(0, seq, tq):
        nb = t0 // SUB
        tasks = [(nb + jj, u, u == jj) for jj in reversed(range(n_sub)) for u in range(jj, n_sub)]
        tasks += [(j, u, False) for j in reversed(range(nb)) for u in range(n_sub)]
        tasks += [(None, u, False) for u in range(n_sub)]

        def stage_scores(task, t0=t0):
            j, u, _ = task
            k_blk = k_meta if j is None else k_ref[N_META + j * SUB:N_META + (j + 1) * SUB, :]
            c0 = t0 + u * SUB
            return jnp.dot(k_blk, qt_ref[:, c0:c0 + SUB], preferred_element_type=F32)

        def stage_terms(task, z, z_ahead):
            j, _, first = task
            if j is None:
                return terms(z, None, smat_meta, pad_to=LANES, z_ahead=z_ahead)
            return terms(z, _key_lt_query(SUB) if first else None, smat_ref[...], z_ahead=z_ahead)

        def stage_weights(task, w, suffix, tot, suffix_ahead, t0=t0):
            j, u, first = task
            a0 = u * SUB
            e = w - suffix
            if not first:
                run = run_ref[:, a0:a0 + SUB]
                if suffix_ahead is not None:
                    run = run + _zero_row_after(suffix_ahead)
                e = e - run
            a = jnp.exp2(e)
            if first:
                a = jnp.where(_key_lt_query(SUB), a, 0.0)
            if j is None:
                r = N_META + t0 + a0
                o = acc_ref[:, a0:a0 + SUB] + jnp.dot(vt_head, _pad_rows(a.astype(BF16), LANES),
                                                      preferred_element_type=F32)
                o_ref[r:r + SUB, :] = o.T.astype(o_ref.dtype)
                return
            pv = jnp.dot(vt_ref[:, j * SUB:(j + 1) * SUB], a.astype(BF16),
                         preferred_element_type=F32)
            if first:
                acc_ref[:, a0:a0 + SUB] = pv
                run_ref[:, a0:a0 + SUB] = tot
            else:
                acc_ref[:, a0:a0 + SUB] += pv
                run_ref[:, a0:a0 + SUB] += tot

        n = len(tasks)
        zs, ts = {}, {}
        for i in range(n + SB_TERMS_LAG + SB_WEIGHTS_LAG):
            if i < n:
                zs[i] = stage_scores(tasks[i])
            b = i - SB_TERMS_LAG
            c = b - SB_WEIGHTS_LAG
            if 0 <= b < n:
                ts[b] = stage_terms(tasks[b], zs.pop(b), zs.get(b + 1))
            if 0 <= c:
                ahead = ts.get(c + 1)
                stage_weights(tasks[c], *ts.pop(c), None if ahead is None else ahead[1])


def _sb_attention(q, kv, batch):
    t = q.shape[0]
    n_heads = q.shape[1] // HEAD_DIM
    length = t // batch
    seq = length - N_META
    tq = min(Q_TILE, seq)
    return pl.pallas_call(
        _sb_kernel,
        grid=(batch, n_heads),
        in_specs=[
            pl.BlockSpec((length, HEAD_DIM), lambda b, h: (b, h)),
            pl.BlockSpec((length, HEAD_DIM), lambda b, h: (b, h)),
            pl.BlockSpec((length, HEAD_DIM), lambda b, h: (b, n_heads + h)),
        ],
        out_specs=pl.BlockSpec((length, HEAD_DIM), lambda b, h: (b, h)),
        out_shape=jax.ShapeDtypeStruct((t, n_heads * HEAD_DIM), BF16),
        scratch_shapes=[
            pltpu.VMEM((HEAD_DIM, seq), BF16),
            pltpu.VMEM((HEAD_DIM, seq), BF16),
            pltpu.VMEM((HEAD_DIM, tq), F32),
            pltpu.VMEM((1, tq), F32),
            pltpu.VMEM((SUB, SUB), BF16),
        ],
        compiler_params=pltpu.CompilerParams(
            dimension_semantics=("arbitrary", "arbitrary"), vmem_limit_bytes=VMEM_LIMIT),
        name="sb_attention",
    )(q, kv, kv)


def kernel(x, meta_tokens, norm_attn, norm_mlp, w_up, w_down, fox_w_in, fox_b_f, fox_w_o,
           kv_norm, w_kv, sb_w_q, sb_w_o, final_norm):
    batch, seq, d = x.shape
    length = N_META + seq
    t = batch * length
    assert d == D_MODEL and t % ROW_TILE == 0 and seq % SUB == 0

    meta = jnp.broadcast_to(meta_tokens[None].astype(x.dtype), (batch, N_META, d))
    h = jnp.concatenate([meta, x], axis=1).reshape(t, d)

    fox_w_qkv_bf = _cast_leading_columns(fox_w_in, 3 * d, tn=PROJ_TN // 2)
    fox_w_o_bf, sb_w_q_bf, sb_w_o_bf = fox_w_o.astype(BF16), sb_w_q.astype(BF16), sb_w_o.astype(BF16)
    w_up_bf, w_down_bf = w_up.astype(BF16), w_down.astype(BF16)

    kv = None
    for layer in range(DEPTH):
        if layer < N_A_LAYERS:
            i = layer
            w_f = jnp.pad(fox_w_in[i, :, 3 * d:], ((0, 0), (0, LANES - N_HEADS))).astype(BF16)
            b_f = jnp.pad(fox_b_f[i].astype(F32), (0, LANES - N_HEADS)).reshape(1, LANES)
            qkv, f_logit = _norm_matmul(h, norm_attn[layer], fox_w_qkv_bf, w_layer=i, n_out=3 * d,
                                        out_dtype=BF16, tn=PROJ_TN, n_scaled_cols=d,
                                        scale=QK_SCALE_LOG2, w_side=w_f, name="fox_qkv_proj")
            qx, kx = _gate_bias_columns(f_logit.reshape(batch, length, LANES), b_f, N_HEADS)
            o = _fox_attention(qkv, qx, kx, batch)
            h = _matmul_residual(o, fox_w_o_bf, i, h, tn=PROJ_TN, name="fox_out_proj")
        else:
            i = layer - N_A_LAYERS
            if kv is None:
                kv = _norm_matmul(h, kv_norm, w_kv.astype(BF16), n_out=2 * d, out_dtype=BF16,
                                  tn=PROJ_TN, name="shared_kv_proj")
            q = _norm_matmul(h, norm_attn[layer], sb_w_q_bf, w_layer=i, n_out=d, out_dtype=BF16,
                             tn=PROJ_TN, n_scaled_cols=d, scale=QK_SCALE_LOG2, name="sb_q_proj")
            o = _sb_attention(q, kv, batch)
            h = _matmul_residual(o, sb_w_o_bf, i, h, tn=PROJ_TN, name="sb_out_proj")
        h = _mlp(h, norm_mlp[layer], w_up_bf, w_down_bf, layer, final_norm, tf=1024,
                 final_norm=(layer == DEPTH - 1), name="mlp")
    return h.reshape(batch, length, d)[:, N_META:]
```

```python
import functools

import jax
import jax.numpy as jnp
import numpy as np
from jax import lax
from jax.experimental import pallas as pl
from jax.experimental.pallas import tpu as pltpu

D_MODEL = 2048
N_HEADS = 16
HEAD_DIM = D_MODEL // N_HEADS
D_FF = 4 * D_MODEL
N_META = 16
DEPTH = 4
N_A_LAYERS = DEPTH // 2
RMS_EPS = 1e-6
LOG2E = 1.4426950408889634
QK_SCALE_LOG2 = HEAD_DIM ** -0.5 * LOG2E

LANES = 128
ROW_TILE = 688
LAST_ROW_TILE = 1024
PROJ_TN = 2048
NORM_ROWS = 16
NORM_UNROLL = 8
GATE_BLOCK = 688
SUB = 256
Q_TILE = 1024
ATT_LOOKAHEAD = 4
SB_TERMS_LAG = 6
SB_WEIGHTS_LAG = 4
NEG_BIG = -1e30
VMEM_LIMIT = 56 * 1024 * 1024

F32 = jnp.float32
BF16 = jnp.bfloat16


def _rms_norm_rows(x_ref, g_ref, a_ref):
    n_rows = x_ref.shape[0]
    g = g_ref[...]

    def body(i, _):
        r0 = pl.multiple_of(i * NORM_ROWS, NORM_ROWS)
        x = x_ref[pl.ds(r0, NORM_ROWS), :]
        ms = jnp.mean(x * x, axis=-1, keepdims=True)
        a_ref[pl.ds(r0, NORM_ROWS), :] = (x * lax.rsqrt(ms + RMS_EPS) * g).astype(a_ref.dtype)
        return 0

    lax.fori_loop(0, n_rows // NORM_ROWS, body, 0, unroll=NORM_UNROLL)


def _norm_matmul_kernel(x_ref, g_ref, w_ref, *rest, n_scaled_tiles, scale, with_side):
    if with_side:
        ws_ref, o_ref, side_ref, a_ref = rest
    else:
        o_ref, a_ref = rest
    n = pl.program_id(1)

    @pl.when(n == 0)
    def _():
        _rms_norm_rows(x_ref, g_ref, a_ref)
        if with_side:
            side_ref[...] = jnp.dot(a_ref[...], ws_ref[...], preferred_element_type=F32)

    acc = jnp.dot(a_ref[...], w_ref[...], preferred_element_type=F32)
    if n_scaled_tiles:
        acc = acc * jnp.where(n < n_scaled_tiles, jnp.float32(scale), jnp.float32(1.0))
    o_ref[...] = acc.astype(o_ref.dtype)


def _norm_matmul(h, g, w, *, n_out, out_dtype, tn, n_scaled_cols=0, scale=1.0, w_side=None,
                 w_layer=0, name):
    t, d = h.shape
    grid = (t // ROW_TILE, n_out // tn)
    with_side = w_side is not None
    if w.ndim == 3:
        w_spec = pl.BlockSpec((None, d, tn), lambda m, n: (w_layer, 0, n))
    else:
        w_spec = pl.BlockSpec((d, tn), lambda m, n: (0, n))
    in_specs = [
        pl.BlockSpec((ROW_TILE, d), lambda m, n: (m, 0)),
        pl.BlockSpec((1, d), lambda m, n: (0, 0)),
        w_spec,
    ]
    out_specs = pl.BlockSpec((ROW_TILE, tn), lambda m, n: (m, n))
    out_shape = jax.ShapeDtypeStruct((t, n_out), out_dtype)
    args = (h, g.reshape(1, d), w)
    if with_side:
        n_side = w_side.shape[1]
        in_specs.append(pl.BlockSpec((d, n_side), lambda m, n: (0, 0)))
        out_specs = (out_specs, pl.BlockSpec((ROW_TILE, n_side), lambda m, n: (m, 0)))
        out_shape = (out_shape, jax.ShapeDtypeStruct((t, n_side), F32))
        args += (w_side,)
    return pl.pallas_call(
        functools.partial(_norm_matmul_kernel, n_scaled_tiles=n_scaled_cols // tn, scale=scale,
                          with_side=with_side),
        grid=grid,
        in_specs=in_specs,
        out_specs=out_specs,
        out_shape=out_shape,
        scratch_shapes=[pltpu.VMEM((ROW_TILE, d), BF16)],
        compiler_params=pltpu.CompilerParams(
            dimension_semantics=("arbitrary", "arbitrary"), vmem_limit_bytes=VMEM_LIMIT),
        name=name,
    )(*args)


def _matmul_residual_kernel(o_ref, w_ref, h_ref, out_ref):
    out_ref[...] = h_ref[...] + jnp.dot(o_ref[...], w_ref[...], preferred_element_type=F32)


def _matmul_residual(o, w, layer, h, *, tn, name):
    t, d = o.shape
    n_out = w.shape[2]
    grid = (n_out // tn, t // ROW_TILE)
    return pl.pallas_call(
        _matmul_residual_kernel,
        grid=grid,
        in_specs=[
            pl.BlockSpec((ROW_TILE, d), lambda n, m: (m, 0)),
            pl.BlockSpec((None, d, tn), lambda n, m: (layer, 0, n)),
            pl.BlockSpec((ROW_TILE, tn), lambda n, m: (m, n)),
        ],
        out_specs=pl.BlockSpec((ROW_TILE, tn), lambda n, m: (m, n)),
        out_shape=jax.ShapeDtypeStruct((t, n_out), F32),
        compiler_params=pltpu.CompilerParams(
            dimension_semantics=("arbitrary", "arbitrary"), vmem_limit_bytes=VMEM_LIMIT),
        name=name,
    )(o, w, h)


def _mlp_kernel(x_ref, g_ref, wu_ref, wd_ref, gf_ref, o_ref, a_ref, *, final_norm):
    if len(x_ref.shape) == 3:
        x_ref = x_ref.at[0]
    f = pl.program_id(1)

    @pl.when(f == 0)
    def _():
        _rms_norm_rows(x_ref, g_ref, a_ref)
        o_ref[...] = x_ref[...]

    u = jnp.dot(a_ref[...], wu_ref[...], preferred_element_type=F32)
    u = jnp.square(jnp.maximum(u, 0.0)).astype(BF16)
    o_ref[...] += jnp.dot(u, wd_ref[...], preferred_element_type=F32)

    if final_norm:
        @pl.when(f == pl.num_programs(1) - 1)
        def _():
            gf = gf_ref[...]
            for r0 in range(0, o_ref.shape[0], NORM_ROWS):
                y = o_ref[r0:r0 + NORM_ROWS, :]
                ms = jnp.mean(y * y, axis=-1, keepdims=True)
                o_ref[r0:r0 + NORM_ROWS, :] = y * lax.rsqrt(ms + RMS_EPS) * gf


def _mlp(h, g, w_up, w_down, layer, g_final, *, tf, name):
    t, d = h.shape
    d_ff = w_up.shape[2]
    grid = (t // ROW_TILE, d_ff // tf)
    return pl.pallas_call(
        functools.partial(_mlp_kernel, final_norm=False),
        grid=grid,
        in_specs=[
            pl.BlockSpec((ROW_TILE, d), lambda m, f: (m, 0)),
            pl.BlockSpec((1, d), lambda m, f: (0, 0)),
            pl.BlockSpec((None, d, tf), lambda m, f: (layer, 0, f)),
            pl.BlockSpec((None, tf, d), lambda m, f: (layer, f, 0)),
            pl.BlockSpec((1, d), lambda m, f: (0, 0)),
        ],
        out_specs=pl.BlockSpec((ROW_TILE, d), lambda m, f: (m, 0)),
        out_shape=jax.ShapeDtypeStruct((t, d), F32),
        scratch_shapes=[pltpu.VMEM((ROW_TILE, d), BF16)],
        compiler_params=pltpu.CompilerParams(
            dimension_semantics=("arbitrary", "arbitrary"), vmem_limit_bytes=VMEM_LIMIT),
        name=name,
    )(h, g.reshape(1, d), w_up, w_down, g_final.reshape(1, d))


def _last_mlp(h, g, w_up, w_down, layer, g_final, *, batch, tf, name):
    t, d = h.shape
    length = t // batch
    seq = length - N_META
    d_ff = w_up.shape[2]
    tiles = seq // LAST_ROW_TILE
    grid = (batch * tiles, d_ff // tf)
    return pl.pallas_call(
        functools.partial(_mlp_kernel, final_norm=True),
        grid=grid,
        in_specs=[
            pl.BlockSpec((pl.Element(1), pl.Element(LAST_ROW_TILE), pl.Element(d)),
                         lambda m, f: (m // tiles, pl.multiple_of(
                             N_META + (m % tiles) * LAST_ROW_TILE, N_META), 0)),
            pl.BlockSpec((1, d), lambda m, f: (0, 0)),
            pl.BlockSpec((None, d, tf), lambda m, f: (layer, 0, f)),
            pl.BlockSpec((None, tf, d), lambda m, f: (layer, f, 0)),
            pl.BlockSpec((1, d), lambda m, f: (0, 0)),
        ],
        out_specs=pl.BlockSpec((None, LAST_ROW_TILE, d), lambda m, f: (m // tiles, m % tiles, 0)),
        out_shape=jax.ShapeDtypeStruct((batch, seq, d), F32),
        scratch_shapes=[pltpu.VMEM((LAST_ROW_TILE, d), BF16)],
        compiler_params=pltpu.CompilerParams(
            dimension_semantics=("arbitrary", "arbitrary"), vmem_limit_bytes=VMEM_LIMIT),
        name=name,
    )(h.reshape(batch, length, d), g.reshape(1, d), w_up, w_down, g_final.reshape(1, d))


def _split3_bf16(x):
    hi = x.astype(BF16)
    r = x - hi.astype(F32)
    mid = r.astype(BF16)
    lo = (r - mid.astype(F32)).astype(BF16)
    return hi, mid, lo


def _gate_selectors(n_heads):
    n_parts = 3
    width = n_heads * LANES
    sel = np.zeros((n_parts * LANES, 2 * width), np.float32)
    ones = np.zeros((1, 2 * width), np.float32)
    for h in range(n_heads):
        for p in range(n_parts):
            sel[p * LANES + h, h * LANES + p] = 1.0
            sel[p * LANES + h, width + h * LANES + n_parts + p] = -1.0
            ones[0, h * LANES + n_parts + p] = 1.0
            ones[0, width + h * LANES + p] = 1.0
    return jnp.asarray(sel, BF16), jnp.asarray(ones, F32)


def _gate_kernel(f_ref, b_ref, sel_ref, ones_ref, qx_ref, kx_ref, carry_ref):
    rows = f_ref.shape[1]
    width = qx_ref.shape[1]

    @pl.when(pl.program_id(1) == 0)
    def _():
        carry_ref[...] = jnp.zeros_like(carry_ref)

    x = f_ref[0] + b_ref[...]
    logf = jnp.minimum(x, 0.0) - jnp.log1p(jnp.exp(-jnp.abs(x)))
    ri = lax.broadcasted_iota(jnp.int32, (rows, rows), 0)
    ci = lax.broadcasted_iota(jnp.int32, (rows, rows), 1)
    tri = jnp.where(ci <= ri, 1.0, 0.0).astype(BF16)
    hi, mid, lo = _split3_bf16(logf)
    c = (jnp.dot(tri, hi, preferred_element_type=F32)
         + jnp.dot(tri, mid, preferred_element_type=F32)
         + jnp.dot(tri, lo, preferred_element_type=F32)) + carry_ref[...]
    carry_ref[...] = c[rows - 1:rows, :]
    parts = jnp.concatenate(_split3_bf16(c * LOG2E), axis=1)
    out = jnp.dot(parts, sel_ref[...], preferred_element_type=F32) + ones_ref[...]
    qx_ref[...] = out[:, :width].astype(qx_ref.dtype)
    kx_ref[...] = out[:, width:].astype(kx_ref.dtype)


def _gate_bias_columns(f_logit, bias, n_heads):
    b, length, lanes = f_logit.shape
    n_blk = length // GATE_BLOCK
    width = n_heads * LANES
    sel, ones = _gate_selectors(n_heads)
    out = jax.ShapeDtypeStruct((b * length, width), BF16)
    out_spec = pl.BlockSpec((GATE_BLOCK, width), lambda i, r: (i * n_blk + r, 0))
    return pl.pallas_call(
        _gate_kernel,
        grid=(b, n_blk),
        in_specs=[
            pl.BlockSpec((1, GATE_BLOCK, lanes), lambda i, r: (i, r, 0)),
            pl.BlockSpec((1, lanes), lambda i, r: (0, 0)),
            pl.BlockSpec(sel.shape, lambda i, r: (0, 0)),
            pl.BlockSpec(ones.shape, lambda i, r: (0, 0)),
        ],
        out_specs=(out_spec, out_spec),
        out_shape=(out, out),
        scratch_shapes=[pltpu.VMEM((1, lanes), F32)],
        compiler_params=pltpu.CompilerParams(
            dimension_semantics=("arbitrary", "arbitrary"), vmem_limit_bytes=VMEM_LIMIT),
        name="gate_cumsum",
    )(f_logit, bias, sel, ones)


def _nt(a, b):
    return lax.dot_general(a, b, (((1,), (1,)), ((), ())), preferred_element_type=F32)


def _pad_rows(x, rows):
    return jnp.concatenate([x, jnp.zeros((rows - x.shape[0], x.shape[1]), x.dtype)], axis=0)


def _key_le_query(n):
    ki = lax.broadcasted_iota(jnp.int32, (n, n), 0)
    qi = lax.broadcasted_iota(jnp.int32, (n, n), 1)
    return ki <= qi


def _key_lt_query(n):
    ki = lax.broadcasted_iota(jnp.int32, (n, n), 0)
    qi = lax.broadcasted_iota(jnp.int32, (n, n), 1)
    return ki < qi


def _fox_kernel(q_ref, k_ref, v_ref, qx_ref, kx_ref, o_ref, vt_ref, qt_ref, acc_ref, m_ref, l_ref):
    length = q_ref.shape[0]
    seq = length - N_META
    tq = min(Q_TILE, seq)
    n_sub = tq // SUB

    def q_rows(r0, rows):
        return jnp.concatenate([q_ref[r0:r0 + rows, :], qx_ref[r0:r0 + rows, :]], axis=1)

    def k_rows(r0, rows):
        return jnp.concatenate([k_ref[r0:r0 + rows, :], kx_ref[r0:r0 + rows, :]], axis=1)

    vt_ref[...] = v_ref[N_META:, :].T
    vt_head = v_ref[0:LANES, :].T
    qt_ref[0:HEAD_DIM, :] = q_ref[N_META:, :].T
    qt_ref[HEAD_DIM:, :] = qx_ref[N_META:, :].T

    s = _nt(k_rows(0, LANES), q_rows(0, LANES))
    s = jnp.where(_key_le_query(LANES), s, NEG_BIG)
    m = jnp.max(s, axis=0, keepdims=True)
    p = jnp.exp2(s - m)
    l = jnp.sum(p, axis=0, keepdims=True)
    acc = jnp.dot(vt_head, p.astype(BF16), preferred_element_type=F32)
    o_ref[0:N_META, :] = (acc * (1.0 / l)).T[0:N_META, :].astype(o_ref.dtype)

    tasks = []
    for t0 in range(0, seq, tq):
        nb = t0 // SUB
        tasks += [(None, nb + u, False) for u in range(n_sub)]
        tasks += [(j, nb + u, False) for j in range(nb) for u in range(n_sub)]
        tasks += [(nb + jj, nb + u, u == jj) for jj in range(n_sub) for u in range(jj, n_sub)]

    def scores(task):
        j, c, _ = task
        k_blk = k_rows(0, N_META) if j is None else k_rows(N_META + j * SUB, SUB)
        return jnp.dot(k_blk, qt_ref[:, c * SUB:(c + 1) * SUB], preferred_element_type=F32)

    def consume(task, s, s_ahead):
        j, c, masked = task
        a0 = c * SUB
        if j is None:
            m = jnp.max(s, axis=0, keepdims=True)
            p = jnp.exp2(s - m)
            m_ref[:, a0:a0 + SUB] = m
            l_ref[:, a0:a0 + SUB] = jnp.sum(p, axis=0, keepdims=True)
            acc_ref[:, a0:a0 + SUB] = jnp.dot(vt_head, _pad_rows(p.astype(BF16), LANES),
                                              preferred_element_type=F32)
            return
        if masked:
            s = jnp.where(_key_le_query(SUB), s, NEG_BIG)
        m_old = m_ref[:, a0:a0 + SUB]
        m_new = jnp.maximum(m_old, jnp.max(s, axis=0, keepdims=True))
        if s_ahead is not None:
            m_new = m_new + _zero_row_after(s_ahead)
        alpha = jnp.exp2(m_old - m_new)
        p = jnp.exp2(s - m_new)
        l = alpha * l_ref[:, a0:a0 + SUB] + jnp.sum(p, axis=0, keepdims=True)
        pv = jnp.dot(vt_ref[:, j * SUB:(j + 1) * SUB], p.astype(BF16),
                     preferred_element_type=F32)
        acc = alpha * acc_ref[:, a0:a0 + SUB] + pv
        if masked:
            r = N_META + a0
            o_ref[r:r + SUB, :] = (acc * (1.0 / l)).T.astype(o_ref.dtype)
        else:
            m_ref[:, a0:a0 + SUB] = m_new
            l_ref[:, a0:a0 + SUB] = l
            acc_ref[:, a0:a0 + SUB] = acc

    pending = {}
    for i in range(len(tasks) + ATT_LOOKAHEAD):
        if i < len(tasks):
            pending[i] = scores(tasks[i])
        if i >= ATT_LOOKAHEAD:
            consume(tasks[i - ATT_LOOKAHEAD], pending.pop(i - ATT_LOOKAHEAD),
                    pending.get(i - ATT_LOOKAHEAD + 1))


def _fox_attention(qkv, qx, kx, batch):
    t = qkv.shape[0]
    n_heads = qkv.shape[1] // (3 * HEAD_DIM)
    length = t // batch
    seq = length - N_META
    tq = min(Q_TILE, seq)
    return pl.pallas_call(
        _fox_kernel,
        grid=(batch, n_heads),
        in_specs=[
            pl.BlockSpec((length, HEAD_DIM), lambda b, h: (b, h)),
            pl.BlockSpec((length, HEAD_DIM), lambda b, h: (b, n_heads + h)),
            pl.BlockSpec((length, HEAD_DIM), lambda b, h: (b, 2 * n_heads + h)),
            pl.BlockSpec((length, LANES), lambda b, h: (b, h)),
            pl.BlockSpec((length, LANES), lambda b, h: (b, h)),
        ],
        out_specs=pl.BlockSpec((length, HEAD_DIM), lambda b, h: (b, h)),
        out_shape=jax.ShapeDtypeStruct((t, n_heads * HEAD_DIM), BF16),
        scratch_shapes=[
            pltpu.VMEM((HEAD_DIM, seq), BF16),
            pltpu.VMEM((HEAD_DIM + LANES, seq), BF16),
            pltpu.VMEM((HEAD_DIM, seq), F32),
            pltpu.VMEM((1, seq), F32),
            pltpu.VMEM((1, seq), F32),
        ],
        compiler_params=pltpu.CompilerParams(
            dimension_semantics=("arbitrary", "arbitrary"), vmem_limit_bytes=VMEM_LIMIT),
        name="fox_attention",
    )(qkv, qkv, qkv, qx, kx)


def _suffix_matrix(rows, width):
    si = lax.broadcasted_iota(jnp.int32, (rows, width), 0)
    ji = lax.broadcasted_iota(jnp.int32, (rows, width), 1)
    return jnp.where((ji > si) & (ji < rows), 1.0, 0.0).astype(BF16)


def _zero_row_after(x):
    return jnp.minimum(jnp.maximum(x[0:1, :], 0.0), 0.0)


def _sb_kernel(q_ref, k_ref, v_ref, o_ref, vt_ref, qt_ref, acc_ref, run_ref, smat_ref):
    length = q_ref.shape[0]
    seq = length - N_META
    tq = min(Q_TILE, seq)
    n_sub = tq // SUB

    vt_ref[...] = v_ref[N_META:, :].T
    qt_ref[...] = q_ref[N_META:, :].T
    vt_head = v_ref[0:LANES, :].T
    k_meta = k_ref[0:N_META, :]
    smat_ref[...] = _suffix_matrix(SUB, SUB)
    smat_meta = _suffix_matrix(N_META, LANES)

    def terms(z, mask, suffix_mat, pad_to=None, z_ahead=None):
        one = 1.0 if z_ahead is None else 1.0 + _zero_row_after(z_ahead)
        pos = jnp.maximum(z, 0.0) + jnp.log2(one + jnp.exp2(-jnp.abs(z)))
        if mask is not None:
            pos = jnp.where(mask, pos, 0.0)
        hi = pos.astype(BF16)
        if pad_to is not None:
            hi = _pad_rows(hi, pad_to)
        suffix = jnp.dot(suffix_mat, hi, preferred_element_type=F32)
        return z - pos, suffix, jnp.sum(pos, axis=0, keepdims=True)

    z = _nt(k_ref[0:LANES, :], q_ref[0:LANES, :])
    mask = _key_lt_query(LANES)
    w, suffix, _ = terms(z, mask, _suffix_matrix(LANES, LANES))
    a = jnp.where(mask, jnp.exp2(w - suffix), 0.0)
    acc = jnp.dot(vt_head, a.astype(BF16), preferred_element_type=F32)
    o_ref[0:N_META, :] = acc.T[0:N_META, :].astype(o_ref.dtype)

    for t0 in range(0, seq, tq):
        nb = t0 // SUB
        tasks = [(nb + jj, u, u == jj) for jj in reversed(range(n_sub)) for u in range(jj, n_sub)]
        tasks += [(j, u, False) for j in reversed(range(nb)) for u in range(n_sub)]
        tasks += [(None, u, False) for u in range(n_sub)]

        def stage_scores(task, t0=t0):
            j, u, _ = task
            k_blk = k_meta if j is None else k_ref[N_META + j * SUB:N_META + (j + 1) * SUB, :]
            c0 = t0 + u * SUB
            return jnp.dot(k_blk, qt_ref[:, c0:c0 + SUB], preferred_element_type=F32)

        def stage_terms(task, z, z_ahead):
            j, _, first = task
            if j is None:
                return terms(z, None, smat_meta, pad_to=LANES, z_ahead=z_ahead)
            return terms(z, _key_lt_query(SUB) if first else None, smat_ref[...], z_ahead=z_ahead)

        def stage_weights(task, w, suffix, tot, suffix_ahead, t0=t0):
            j, u, first = task
            a0 = u * SUB
            e = w - suffix
            if not first:
                run = run_ref[:, a0:a0 + SUB]
                if suffix_ahead is not None:
                    run = run + _zero_row_after(suffix_ahead)
                e = e - run
            a = jnp.exp2(e)
            if first:
                a = jnp.where(_key_lt_query(SUB), a, 0.0)
            if j is None:
                r = N_META + t0 + a0
                o = acc_ref[:, a0:a0 + SUB] + jnp.dot(vt_head, _pad_rows(a.astype(BF16), LANES),
                                                      preferred_element_type=F32)
                o_ref[r:r + SUB, :] = o.T.astype(o_ref.dtype)
                return
            pv = jnp.dot(vt_ref[:, j * SUB:(j + 1) * SUB], a.astype(BF16),
                         preferred_element_type=F32)
            if first:
                acc_ref[:, a0:a0 + SUB] = pv
                run_ref[:, a0:a0 + SUB] = tot
            else:
                acc_ref[:, a0:a0 + SUB] += pv
                run_ref[:, a0:a0 + SUB] += tot

        n = len(tasks)
        zs, ts = {}, {}
        for i in range(n + SB_TERMS_LAG + SB_WEIGHTS_LAG):
            b = i - SB_TERMS_LAG
            c = b - SB_WEIGHTS_LAG
            if i < n:
                zs[i] = stage_scores(tasks[i])
            if 0 <= b < n:
                ts[b] = stage_terms(tasks[b], zs.pop(b), zs.get(b + 1))
            if 0 <= c:
                ahead = ts.get(c + 1)
                stage_weights(tasks[c], *ts.pop(c), None if ahead is None else ahead[1])


def _sb_attention(q, kv, batch):
    t = q.shape[0]
    n_heads = q.shape[1] // HEAD_DIM
    length = t // batch
    seq = length - N_META
    tq = min(Q_TILE, seq)
    return pl.pallas_call(
        _sb_kernel,
        grid=(batch, n_heads),
        in_specs=[
            pl.BlockSpec((length, HEAD_DIM), lambda b, h: (b, h)),
            pl.BlockSpec((length, HEAD_DIM), lambda b, h: (b, h)),
            pl.BlockSpec((length, HEAD_DIM), lambda b, h: (b, n_heads + h)),
        ],
        out_specs=pl.BlockSpec((length, HEAD_DIM), lambda b, h: (b, h)),
        out_shape=jax.ShapeDtypeStruct((t, n_heads * HEAD_DIM), BF16),
        scratch_shapes=[
            pltpu.VMEM((HEAD_DIM, seq), BF16),
            pltpu.VMEM((HEAD_DIM, seq), BF16),
            pltpu.VMEM((HEAD_DIM, tq), F32),
            pltpu.VMEM((1, tq), F32),
            pltpu.VMEM((SUB, SUB), BF16),
        ],
        compiler_params=pltpu.CompilerParams(
            dimension_semantics=("arbitrary", "arbitrary"), vmem_limit_bytes=VMEM_LIMIT),
        name="sb_attention",
    )(q, kv, kv)


def kernel(x, meta_tokens, norm_attn, norm_mlp, w_up, w_down, fox_w_in, fox_b_f, fox_w_o,
           kv_norm, w_kv, sb_w_q, sb_w_o, final_norm):
    batch, seq, d = x.shape
    length = N_META + seq
    t = batch * length
    assert d == D_MODEL and t % ROW_TILE == 0 and seq % SUB == 0

    meta = jnp.broadcast_to(meta_tokens[None].astype(x.dtype), (batch, N_META, d))
    h = jnp.concatenate([meta, x], axis=1).reshape(t, d)

    fox_w_in_bf = fox_w_in.astype(BF16)
    fox_w_o_bf, sb_w_q_bf, sb_w_o_bf = fox_w_o.astype(BF16), sb_w_q.astype(BF16), sb_w_o.astype(BF16)
    w_up_bf, w_down_bf = w_up.astype(BF16), w_down.astype(BF16)

    kv = None
    for layer in range(DEPTH):
        if layer < N_A_LAYERS:
            i = layer
            w_f = jnp.pad(fox_w_in_bf[i, :, 3 * d:], ((0, 0), (0, LANES - N_HEADS)))
            b_f = jnp.pad(fox_b_f[i].astype(F32), (0, LANES - N_HEADS)).reshape(1, LANES)
            qkv, f_logit = _norm_matmul(h, norm_attn[layer], fox_w_in_bf, w_layer=i, n_out=3 * d,
                                        out_dtype=BF16, tn=PROJ_TN, n_scaled_cols=d,
                                        scale=QK_SCALE_LOG2, w_side=w_f, name="fox_qkv_proj")
            qx, kx = _gate_bias_columns(f_logit.reshape(batch, length, LANES), b_f, N_HEADS)
            o = _fox_attention(qkv, qx, kx, batch)
            h = _matmul_residual(o, fox_w_o_bf, i, h, tn=PROJ_TN, name="fox_out_proj")
        else:
            i = layer - N_A_LAYERS
            if kv is None:
                kv = _norm_matmul(h, kv_norm, w_kv.astype(BF16), n_out=2 * d, out_dtype=BF16,
                                  tn=PROJ_TN, name="shared_kv_proj")
            q = _norm_matmul(h, norm_attn[layer], sb_w_q_bf, w_layer=i, n_out=d, out_dtype=BF16,
                             tn=PROJ_TN, n_scaled_cols=d, scale=QK_SCALE_LOG2, name="sb_q_proj")
            o = _sb_attention(q, kv, batch)
            h = _matmul_residual(o, sb_w_o_bf, i, h, tn=PROJ_TN, name="sb_out_proj")
        if layer < DEPTH - 1:
            h = _mlp(h, norm_mlp[layer], w_up_bf, w_down_bf, layer, final_norm, tf=1024, name="mlp")
    return _last_mlp(h, norm_mlp[DEPTH - 1], w_up_bf, w_down_bf, DEPTH - 1, final_norm,
                     batch=batch, tf=512, name="last_mlp")
```

```python
import functools

import jax
import jax.numpy as jnp
import numpy as np
from jax import lax
from jax.experimental import pallas as pl
from jax.experimental.pallas import tpu as pltpu

D_MODEL = 2048
N_HEADS = 16
HEAD_DIM = D_MODEL // N_HEADS
D_FF = 4 * D_MODEL
N_META = 16
DEPTH = 4
N_A_LAYERS = DEPTH // 2
RMS_EPS = 1e-6
LOG2E = 1.4426950408889634
QK_SCALE_LOG2 = HEAD_DIM ** -0.5 * LOG2E

LANES = 128
ROW_TILE = 688
LAST_ROW_TILE = 1024
PROJ_TN = 2048
NORM_ROWS = 16
NORM_UNROLL = 8
GATE_BLOCK = 688
SUB = 256
Q_TILE = 1024
ATT_LOOKAHEAD = 4
SB_TERMS_LAG = 6
SB_WEIGHTS_LAG = 3
NEG_BIG = -1e30
VMEM_LIMIT = 56 * 1024 * 1024

F32 = jnp.float32
BF16 = jnp.bfloat16


def _rms_norm_rows(x_ref, g_ref, a_ref):
    n_rows = x_ref.shape[0]
    g = g_ref[...]

    def body(i, _):
        r0 = pl.multiple_of(i * NORM_ROWS, NORM_ROWS)
        x = x_ref[pl.ds(r0, NORM_ROWS), :]
        ms = jnp.mean(x * x, axis=-1, keepdims=True)
        a_ref[pl.ds(r0, NORM_ROWS), :] = (x * lax.rsqrt(ms + RMS_EPS) * g).astype(a_ref.dtype)
        return 0

    lax.fori_loop(0, n_rows // NORM_ROWS, body, 0, unroll=NORM_UNROLL)


def _norm_matmul_kernel(x_ref, g_ref, w_ref, *rest, n_scaled_tiles, scale, with_side):
    if with_side:
        ws_ref, o_ref, side_ref, a_ref = rest
    else:
        o_ref, a_ref = rest
    n = pl.program_id(1)

    @pl.when(n == 0)
    def _():
        _rms_norm_rows(x_ref, g_ref, a_ref)
        if with_side:
            side_ref[...] = jnp.dot(a_ref[...], ws_ref[...], preferred_element_type=F32)

    acc = jnp.dot(a_ref[...], w_ref[...], preferred_element_type=F32)
    if n_scaled_tiles:
        acc = acc * jnp.where(n < n_scaled_tiles, jnp.float32(scale), jnp.float32(1.0))
    o_ref[...] = acc.astype(o_ref.dtype)


def _norm_matmul(h, g, w, *, n_out, out_dtype, tn, n_scaled_cols=0, scale=1.0, w_side=None,
                 w_layer=0, name):
    t, d = h.shape
    grid = (t // ROW_TILE, n_out // tn)
    with_side = w_side is not None
    if w.ndim == 3:
        w_spec = pl.BlockSpec((None, d, tn), lambda m, n: (w_layer, 0, n))
    else:
        w_spec = pl.BlockSpec((d, tn), lambda m, n: (0, n))
    in_specs = [
        pl.BlockSpec((ROW_TILE, d), lambda m, n: (m, 0)),
        pl.BlockSpec((1, d), lambda m, n: (0, 0)),
        w_spec,
    ]
    out_specs = pl.BlockSpec((ROW_TILE, tn), lambda m, n: (m, n))
    out_shape = jax.ShapeDtypeStruct((t, n_out), out_dtype)
    args = (h, g.reshape(1, d), w)
    if with_side:
        n_side = w_side.shape[1]
        in_specs.append(pl.BlockSpec((d, n_side), lambda m, n: (0, 0)))
        out_specs = (out_specs, pl.BlockSpec((ROW_TILE, n_side), lambda m, n: (m, 0)))
        out_shape = (out_shape, jax.ShapeDtypeStruct((t, n_side), F32))
        args += (w_side,)
    return pl.pallas_call(
        functools.partial(_norm_matmul_kernel, n_scaled_tiles=n_scaled_cols // tn, scale=scale,
                          with_side=with_side),
        grid=grid,
        in_specs=in_specs,
        out_specs=out_specs,
        out_shape=out_shape,
        scratch_shapes=[pltpu.VMEM((ROW_TILE, d), BF16)],
        compiler_params=pltpu.CompilerParams(
            dimension_semantics=("arbitrary", "arbitrary"), vmem_limit_bytes=VMEM_LIMIT),
        name=name,
    )(*args)


def _matmul_residual_kernel(o_ref, w_ref, h_ref, out_ref):
    out_ref[...] = h_ref[...] + jnp.dot(o_ref[...], w_ref[...], preferred_element_type=F32)


def _matmul_residual(o, w, layer, h, *, tn, name):
    t, d = o.shape
    n_out = w.shape[2]
    grid = (n_out // tn, t // ROW_TILE)
    return pl.pallas_call(
        _matmul_residual_kernel,
        grid=grid,
        in_specs=[
            pl.BlockSpec((ROW_TILE, d), lambda n, m: (m, 0)),
            pl.BlockSpec((None, d, tn), lambda n, m: (layer, 0, n)),
            pl.BlockSpec((ROW_TILE, tn), lambda n, m: (m, n)),
        ],
        out_specs=pl.BlockSpec((ROW_TILE, tn), lambda n, m: (m, n)),
        out_shape=jax.ShapeDtypeStruct((t, n_out), F32),
        compiler_params=pltpu.CompilerParams(
            dimension_semantics=("arbitrary", "arbitrary"), vmem_limit_bytes=VMEM_LIMIT),
        name=name,
    )(o, w, h)


def _mlp_kernel(x_ref, g_ref, wu_ref, wd_ref, gf_ref, o_ref, a_ref, *, final_norm):
    if len(x_ref.shape) == 3:
        x_ref = x_ref.at[0]
    f = pl.program_id(1)

    @pl.when(f == 0)
    def _():
        _rms_norm_rows(x_ref, g_ref, a_ref)
        o_ref[...] = x_ref[...]

    u = jnp.dot(a_ref[...], wu_ref[...], preferred_element_type=F32)
    u = jnp.square(jnp.maximum(u, 0.0)).astype(BF16)
    o_ref[...] += jnp.dot(u, wd_ref[...], preferred_element_type=F32)

    if final_norm:
        @pl.when(f == pl.num_programs(1) - 1)
        def _():
            gf = gf_ref[...]
            for r0 in range(0, o_ref.shape[0], NORM_ROWS):
                y = o_ref[r0:r0 + NORM_ROWS, :]
                ms = jnp.mean(y * y, axis=-1, keepdims=True)
                o_ref[r0:r0 + NORM_ROWS, :] = y * lax.rsqrt(ms + RMS_EPS) * gf


def _mlp(h, g, w_up, w_down, layer, g_final, *, tf, name):
    t, d = h.shape
    d_ff = w_up.shape[2]
    grid = (t // ROW_TILE, d_ff // tf)
    return pl.pallas_call(
        functools.partial(_mlp_kernel, final_norm=False),
        grid=grid,
        in_specs=[
            pl.BlockSpec((ROW_TILE, d), lambda m, f: (m, 0)),
            pl.BlockSpec((1, d), lambda m, f: (0, 0)),
            pl.BlockSpec((None, d, tf), lambda m, f: (layer, 0, f)),
            pl.BlockSpec((None, tf, d), lambda m, f: (layer, f, 0)),
            pl.BlockSpec((1, d), lambda m, f: (0, 0)),
        ],
        out_specs=pl.BlockSpec((ROW_TILE, d), lambda m, f: (m, 0)),
        out_shape=jax.ShapeDtypeStruct((t, d), F32),
        scratch_shapes=[pltpu.VMEM((ROW_TILE, d), BF16)],
        compiler_params=pltpu.CompilerParams(
            dimension_semantics=("arbitrary", "arbitrary"), vmem_limit_bytes=VMEM_LIMIT),
        name=name,
    )(h, g.reshape(1, d), w_up, w_down, g_final.reshape(1, d))


def _last_mlp(h, g, w_up, w_down, layer, g_final, *, batch, tf, name):
    t, d = h.shape
    length = t // batch
    seq = length - N_META
    d_ff = w_up.shape[2]
    tiles = seq // LAST_ROW_TILE
    grid = (batch * tiles, d_ff // tf)
    return pl.pallas_call(
        functools.partial(_mlp_kernel, final_norm=True),
        grid=grid,
        in_specs=[
            pl.BlockSpec((pl.Element(1), pl.Element(LAST_ROW_TILE), pl.Element(d)),
                         lambda m, f: (m // tiles, pl.multiple_of(
                             N_META + (m % tiles) * LAST_ROW_TILE, N_META), 0)),
            pl.BlockSpec((1, d), lambda m, f: (0, 0)),
            pl.BlockSpec((None, d, tf), lambda m, f: (layer, 0, f)),
            pl.BlockSpec((None, tf, d), lambda m, f: (layer, f, 0)),
            pl.BlockSpec((1, d), lambda m, f: (0, 0)),
        ],
        out_specs=pl.BlockSpec((None, LAST_ROW_TILE, d), lambda m, f: (m // tiles, m % tiles, 0)),
        out_shape=jax.ShapeDtypeStruct((batch, seq, d), F32),
        scratch_shapes=[pltpu.VMEM((LAST_ROW_TILE, d), BF16)],
        compiler_params=pltpu.CompilerParams(
            dimension_semantics=("arbitrary", "arbitrary"), vmem_limit_bytes=VMEM_LIMIT),
        name=name,
    )(h.reshape(batch, length, d), g.reshape(1, d), w_up, w_down, g_final.reshape(1, d))


def _split3_bf16(x):
    hi = x.astype(BF16)
    r = x - hi.astype(F32)
    mid = r.astype(BF16)
    lo = (r - mid.astype(F32)).astype(BF16)
    return hi, mid, lo


def _gate_selectors(n_heads):
    n_parts = 3
    width = n_heads * LANES
    sel = np.zeros((n_parts * LANES, 2 * width), np.float32)
    ones = np.zeros((1, 2 * width), np.float32)
    for h in range(n_heads):
        for p in range(n_parts):
            sel[p * LANES + h, h * LANES + p] = 1.0
            sel[p * LANES + h, width + h * LANES + n_parts + p] = -1.0
            ones[0, h * LANES + n_parts + p] = 1.0
            ones[0, width + h * LANES + p] = 1.0
    return jnp.asarray(sel, BF16), jnp.asarray(ones, F32)


def _gate_kernel(f_ref, b_ref, sel_ref, ones_ref, qx_ref, kx_ref, carry_ref):
    rows = f_ref.shape[1]
    width = qx_ref.shape[1]

    @pl.when(pl.program_id(1) == 0)
    def _():
        carry_ref[...] = jnp.zeros_like(carry_ref)

    x = f_ref[0] + b_ref[...]
    logf = jnp.minimum(x, 0.0) - jnp.log1p(jnp.exp(-jnp.abs(x)))
    ri = lax.broadcasted_iota(jnp.int32, (rows, rows), 0)
    ci = lax.broadcasted_iota(jnp.int32, (rows, rows), 1)
    tri = jnp.where(ci <= ri, 1.0, 0.0).astype(BF16)
    hi, mid, lo = _split3_bf16(logf)
    c = (jnp.dot(tri, hi, preferred_element_type=F32)
         + jnp.dot(tri, mid, preferred_element_type=F32)
         + jnp.dot(tri, lo, preferred_element_type=F32)) + carry_ref[...]
    carry_ref[...] = c[rows - 1:rows, :]
    parts = jnp.concatenate(_split3_bf16(c * LOG2E), axis=1)
    out = jnp.dot(parts, sel_ref[...], preferred_element_type=F32) + ones_ref[...]
    qx_ref[...] = out[:, :width].astype(qx_ref.dtype)
    kx_ref[...] = out[:, width:].astype(kx_ref.dtype)


def _gate_bias_columns(f_logit, bias, n_heads):
    b, length, lanes = f_logit.shape
    n_blk = length // GATE_BLOCK
    width = n_heads * LANES
    sel, ones = _gate_selectors(n_heads)
    out = jax.ShapeDtypeStruct((b * length, width), BF16)
    out_spec = pl.BlockSpec((GATE_BLOCK, width), lambda i, r: (i * n_blk + r, 0))
    return pl.pallas_call(
        _gate_kernel,
        grid=(b, n_blk),
        in_specs=[
            pl.BlockSpec((1, GATE_BLOCK, lanes), lambda i, r: (i, r, 0)),
            pl.BlockSpec((1, lanes), lambda i, r: (0, 0)),
            pl.BlockSpec(sel.shape, lambda i, r: (0, 0)),
            pl.BlockSpec(ones.shape, lambda i, r: (0, 0)),
        ],
        out_specs=(out_spec, out_spec),
        out_shape=(out, out),
        scratch_shapes=[pltpu.VMEM((1, lanes), F32)],
        compiler_params=pltpu.CompilerParams(
            dimension_semantics=("arbitrary", "arbitrary"), vmem_limit_bytes=VMEM_LIMIT),
        name="gate_cumsum",
    )(f_logit, bias, sel, ones)


def _nt(a, b):
    return lax.dot_general(a, b, (((1,), (1,)), ((), ())), preferred_element_type=F32)


def _pad_rows(x, rows):
    return jnp.concatenate([x, jnp.zeros((rows - x.shape[0], x.shape[1]), x.dtype)], axis=0)


def _key_le_query(n):
    ki = lax.broadcasted_iota(jnp.int32, (n, n), 0)
    qi = lax.broadcasted_iota(jnp.int32, (n, n), 1)
    return ki <= qi


def _key_lt_query(n):
    ki = lax.broadcasted_iota(jnp.int32, (n, n), 0)
    qi = lax.broadcasted_iota(jnp.int32, (n, n), 1)
    return ki < qi


def _fox_kernel(q_ref, k_ref, v_ref, qx_ref, kx_ref, o_ref, vt_ref, qt_ref, acc_ref, m_ref, l_ref):
    length = q_ref.shape[0]
    seq = length - N_META
    tq = min(Q_TILE, seq)
    n_sub = tq // SUB

    def q_rows(r0, rows):
        return jnp.concatenate([q_ref[r0:r0 + rows, :], qx_ref[r0:r0 + rows, :]], axis=1)

    def k_rows(r0, rows):
        return jnp.concatenate([k_ref[r0:r0 + rows, :], kx_ref[r0:r0 + rows, :]], axis=1)

    vt_ref[...] = v_ref[N_META:, :].T
    vt_head = v_ref[0:LANES, :].T
    qt_ref[0:HEAD_DIM, :] = q_ref[N_META:, :].T
    qt_ref[HEAD_DIM:, :] = qx_ref[N_META:, :].T

    s = _nt(k_rows(0, LANES), q_rows(0, LANES))
    s = jnp.where(_key_le_query(LANES), s, NEG_BIG)
    m = jnp.max(s, axis=0, keepdims=True)
    p = jnp.exp2(s - m)
    l = jnp.sum(p, axis=0, keepdims=True)
    acc = jnp.dot(vt_head, p.astype(BF16), preferred_element_type=F32)
    o_ref[0:N_META, :] = (acc * (1.0 / l)).T[0:N_META, :].astype(o_ref.dtype)

    tasks = []
    for t0 in range(0, seq, tq):
        nb = t0 // SUB
        tasks += [(None, nb + u, False) for u in range(n_sub)]
        tasks += [(j, nb + u, False) for j in range(nb) for u in range(n_sub)]
        tasks += [(nb + jj, nb + u, u == jj) for jj in range(n_sub) for u in range(jj, n_sub)]

    def scores(task):
        j, c, _ = task
        k_blk = k_rows(0, N_META) if j is None else k_rows(N_META + j * SUB, SUB)
        return jnp.dot(k_blk, qt_ref[:, c * SUB:(c + 1) * SUB], preferred_element_type=F32)

    def consume(task, s, s_ahead):
        j, c, masked = task
        a0 = c * SUB
        if j is None:
            m = jnp.max(s, axis=0, keepdims=True)
            p = jnp.exp2(s - m)
            m_ref[:, a0:a0 + SUB] = m
            l_ref[:, a0:a0 + SUB] = jnp.sum(p, axis=0, keepdims=True)
            acc_ref[:, a0:a0 + SUB] = jnp.dot(vt_head, _pad_rows(p.astype(BF16), LANES),
                                              preferred_element_type=F32)
            return
        if masked:
            s = jnp.where(_key_le_query(SUB), s, NEG_BIG)
        m_old = m_ref[:, a0:a0 + SUB]
        m_new = jnp.maximum(m_old, jnp.max(s, axis=0, keepdims=True))
        if s_ahead is not None:
            m_new = m_new + _zero_row_after(s_ahead)
        alpha = jnp.exp2(m_old - m_new)
        p = jnp.exp2(s - m_new)
        l = alpha * l_ref[:, a0:a0 + SUB] + jnp.sum(p, axis=0, keepdims=True)
        pv = jnp.dot(vt_ref[:, j * SUB:(j + 1) * SUB], p.astype(BF16),
                     preferred_element_type=F32)
        acc = alpha * acc_ref[:, a0:a0 + SUB] + pv
        if masked:
            r = N_META + a0
            o_ref[r:r + SUB, :] = (acc * (1.0 / l)).T.astype(o_ref.dtype)
        else:
            m_ref[:, a0:a0 + SUB] = m_new
            l_ref[:, a0:a0 + SUB] = l
            acc_ref[:, a0:a0 + SUB] = acc

    pending = {}
    for i in range(len(tasks) + ATT_LOOKAHEAD):
        if i < len(tasks):
            pending[i] = scores(tasks[i])
        if i >= ATT_LOOKAHEAD:
            consume(tasks[i - ATT_LOOKAHEAD], pending.pop(i - ATT_LOOKAHEAD),
                    pending.get(i - ATT_LOOKAHEAD + 1))


def _fox_attention(qkv, qx, kx, batch):
    t = qkv.shape[0]
    n_heads = qkv.shape[1] // (3 * HEAD_DIM)
    length = t // batch
    seq = length - N_META
    tq = min(Q_TILE, seq)
    return pl.pallas_call(
        _fox_kernel,
        grid=(batch, n_heads),
        in_specs=[
            pl.BlockSpec((length, HEAD_DIM), lambda b, h: (b, h)),
            pl.BlockSpec((length, HEAD_DIM), lambda b, h: (b, n_heads + h)),
            pl.BlockSpec((length, HEAD_DIM), lambda b, h: (b, 2 * n_heads + h)),
            pl.BlockSpec((length, LANES), lambda b, h: (b, h)),
            pl.BlockSpec((length, LANES), lambda b, h: (b, h)),
        ],
        out_specs=pl.BlockSpec((length, HEAD_DIM), lambda b, h: (b, h)),
        out_shape=jax.ShapeDtypeStruct((t, n_heads * HEAD_DIM), BF16),
        scratch_shapes=[
            pltpu.VMEM((HEAD_DIM, seq), BF16),
            pltpu.VMEM((HEAD_DIM + LANES, seq), BF16),
            pltpu.VMEM((HEAD_DIM, seq), F32),
            pltpu.VMEM((1, seq), F32),
            pltpu.VMEM((1, seq), F32),
        ],
        compiler_params=pltpu.CompilerParams(
            dimension_semantics=("arbitrary", "arbitrary"), vmem_limit_bytes=VMEM_LIMIT),
        name="fox_attention",
    )(qkv, qkv, qkv, qx, kx)


def _suffix_matrix(rows, width):
    si = lax.broadcasted_iota(jnp.int32, (rows, width), 0)
    ji = lax.broadcasted_iota(jnp.int32, (rows, width), 1)
    return jnp.where((ji > si) & (ji < rows), 1.0, 0.0).astype(BF16)


def _zero_row_after(x):
    return jnp.minimum(jnp.maximum(x[0:1, :], 0.0), 0.0)


def _sb_kernel(q_ref, k_ref, v_ref, o_ref, vt_ref, qt_ref, acc_ref, run_ref, smat_ref):
    length = q_ref.shape[0]
    seq = length - N_META
    tq = min(Q_TILE, seq)
    n_sub = tq // SUB

    vt_ref[...] = v_ref[N_META:, :].T
    qt_ref[...] = q_ref[N_META:, :].T
    vt_head = v_ref[0:LANES, :].T
    k_meta = k_ref[0:N_META, :]
    smat_ref[...] = _suffix_matrix(SUB, SUB)
    smat_meta = _suffix_matrix(N_META, LANES)

    def terms(z, mask, suffix_mat, pad_to=None, z_ahead=None):
        one = 1.0 if z_ahead is None else 1.0 + _zero_row_after(z_ahead)
        pos = jnp.maximum(z, 0.0) + jnp.log2(one + jnp.exp2(-jnp.abs(z)))
        if mask is not None:
            pos = jnp.where(mask, pos, 0.0)
        hi = pos.astype(BF16)
        tot_first = hi[0:1, :].astype(F32)
        if pad_to is not None:
            hi = _pad_rows(hi, pad_to)
        suffix = jnp.dot(suffix_mat, hi, preferred_element_type=F32)
        return z - pos, suffix, suffix[0:1, :] + tot_first

    z = _nt(k_ref[0:LANES, :], q_ref[0:LANES, :])
    mask = _key_lt_query(LANES)
    w, suffix, _ = terms(z, mask, _suffix_matrix(LANES, LANES))
    a = jnp.where(mask, jnp.exp2(w - suffix), 0.0)
    acc = jnp.dot(vt_head, a.astype(BF16), preferred_element_type=F32)
    o_ref[0:N_META, :] = acc.T[0:N_META, :].astype(o_ref.dtype)

    for t0 in range(0, seq, tq):
        nb = t0 // SUB
        tasks = [(nb + jj, u, u == jj) for jj in reversed(range(n_sub)) for u in range(jj, n_sub)]
        tasks += [(j, u, False) for j in reversed(range(nb)) for u in range(n_sub)]
        tasks += [(None, u, False) for u in range(n_sub)]

        def stage_scores(task, t0=t0):
            j, u, _ = task
            k_blk = k_meta if j is None else k_ref[N_META + j * SUB:N_META + (j + 1) * SUB, :]
            c0 = t0 + u * SUB
            return jnp.dot(k_blk, qt_ref[:, c0:c0 + SUB], preferred_element_type=F32)

        def stage_terms(task, z, z_ahead):
            j, _, first = task
            if j is None:
                return terms(z, None, smat_meta, pad_to=LANES, z_ahead=z_ahead)
            return terms(z, _key_lt_query(SUB) if first else None, smat_ref[...], z_ahead=z_ahead)

        def stage_weights(task, w, suffix, tot, suffix_ahead, t0=t0):
            j, u, first = task
            a0 = u * SUB
            a = jnp.exp2(w - suffix)
            if first:
                a = jnp.where(_key_lt_query(SUB), a, 0.0)
            else:
                run = run_ref[:, a0:a0 + SUB]
                if suffix_ahead is not None:
                    run = run + _zero_row_after(suffix_ahead)
                decay = jnp.exp2(-run)
            if j is None:
                r = N_META + t0 + a0
                pv = jnp.dot(vt_head, _pad_rows(a.astype(BF16), LANES), preferred_element_type=F32)
                o_ref[r:r + SUB, :] = (acc_ref[:, a0:a0 + SUB] + pv * decay).T.astype(o_ref.dtype)
                return
            pv = jnp.dot(vt_ref[:, j * SUB:(j + 1) * SUB], a.astype(BF16),
                         preferred_element_type=F32)
            if first:
                acc_ref[:, a0:a0 + SUB] = pv
                run_ref[:, a0:a0 + SUB] = tot
            else:
                acc_ref[:, a0:a0 + SUB] += pv * decay
                run_ref[:, a0:a0 + SUB] += tot

        n = len(tasks)
        zs, ts = {}, {}
        for i in range(n + SB_TERMS_LAG + SB_WEIGHTS_LAG):
            b = i - SB_TERMS_LAG
            c = b - SB_WEIGHTS_LAG
            if i < n:
                zs[i] = stage_scores(tasks[i])
            if 0 <= b < n:
                ts[b] = stage_terms(tasks[b], zs.pop(b), zs.get(b + 1))
            if 0 <= c:
                ahead = ts.get(c + 1)
                stage_weights(tasks[c], *ts.pop(c), None if ahead is None else ahead[1])


def _sb_attention(q, kv, batch):
    t = q.shape[0]
    n_heads = q.shape[1] // HEAD_DIM
    length = t // batch
    seq = length - N_META
    tq = min(Q_TILE, seq)
    return pl.pallas_call(
        _sb_kernel,
        grid=(batch, n_heads),
        in_specs=[
            pl.BlockSpec((length, HEAD_DIM), lambda b, h: (b, h)),
            pl.BlockSpec((length, HEAD_DIM), lambda b, h: (b, h)),
            pl.BlockSpec((length, HEAD_DIM), lambda b, h: (b, n_heads + h)),
        ],
        out_specs=pl.BlockSpec((length, HEAD_DIM), lambda b, h: (b, h)),
        out_shape=jax.ShapeDtypeStruct((t, n_heads * HEAD_DIM), BF16),
        scratch_shapes=[
            pltpu.VMEM((HEAD_DIM, seq), BF16),
            pltpu.VMEM((HEAD_DIM, seq), BF16),
            pltpu.VMEM((HEAD_DIM, tq), F32),
            pltpu.VMEM((1, tq), F32),
            pltpu.VMEM((SUB, SUB), BF16),
        ],
        compiler_params=pltpu.CompilerParams(
            dimension_semantics=("arbitrary", "arbitrary"), vmem_limit_bytes=VMEM_LIMIT),
        name="sb_attention",
    )(q, kv, kv)


def kernel(x, meta_tokens, norm_attn, norm_mlp, w_up, w_down, fox_w_in, fox_b_f, fox_w_o,
           kv_norm, w_kv, sb_w_q, sb_w_o, final_norm):
    batch, seq, d = x.shape
    length = N_META + seq
    t = batch * length
    assert d == D_MODEL and t % ROW_TILE == 0 and seq % SUB == 0

    meta = jnp.broadcast_to(meta_tokens[None].astype(x.dtype), (batch, N_META, d))
    h = jnp.concatenate([meta, x], axis=1).reshape(t, d)

    fox_w_in_bf = fox_w_in.astype(BF16)
    fox_w_o_bf, sb_w_q_bf, sb_w_o_bf = fox_w_o.astype(BF16), sb_w_q.astype(BF16), sb_w_o.astype(BF16)
    w_up_bf, w_down_bf = w_up.astype(BF16), w_down.astype(BF16)

    kv = None
    for layer in range(DEPTH):
        if layer < N_A_LAYERS:
            i = layer
            w_f = jnp.pad(fox_w_in_bf[i, :, 3 * d:], ((0, 0), (0, LANES - N_HEADS)))
            b_f = jnp.pad(fox_b_f[i].astype(F32), (0, LANES - N_HEADS)).reshape(1, LANES)
            qkv, f_logit = _norm_matmul(h, norm_attn[layer], fox_w_in_bf, w_layer=i, n_out=3 * d,
                                        out_dtype=BF16, tn=PROJ_TN, n_scaled_cols=d,
                                        scale=QK_SCALE_LOG2, w_side=w_f, name="fox_qkv_proj")
            qx, kx = _gate_bias_columns(f_logit.reshape(batch, length, LANES), b_f, N_HEADS)
            o = _fox_attention(qkv, qx, kx, batch)
            h = _matmul_residual(o, fox_w_o_bf, i, h, tn=PROJ_TN, name="fox_out_proj")
        else:
            i = layer - N_A_LAYERS
            if kv is None:
                kv = _norm_matmul(h, kv_norm, w_kv.astype(BF16), n_out=2 * d, out_dtype=BF16,
                                  tn=PROJ_TN, name="shared_kv_proj")
            q = _norm_matmul(h, norm_attn[layer], sb_w_q_bf, w_layer=i, n_out=d, out_dtype=BF16,
                             tn=PROJ_TN, n_scaled_cols=d, scale=QK_SCALE_LOG2, name="sb_q_proj")
            o = _sb_attention(q, kv, batch)
            h = _matmul_residual(o, sb_w_o_bf, i, h, tn=PROJ_TN, name="sb_out_proj")
        if layer < DEPTH - 1:
            h = _mlp(h, norm_mlp[layer], w_up_bf, w_down_bf, layer, final_norm, tf=1024, name="mlp")
    return _last_mlp(h, norm_mlp[DEPTH - 1], w_up_bf, w_down_bf, DEPTH - 1, final_norm,
                     batch=batch, tf=512, name="last_mlp")
```

```python
import functools

import jax
import jax.numpy as jnp
import numpy as np
from jax import lax
from jax.experimental import pallas as pl
from jax.experimental.pallas import tpu as pltpu

D_MODEL = 2048
N_HEADS = 16
HEAD_DIM = D_MODEL // N_HEADS
D_FF = 4 * D_MODEL
N_META = 16
DEPTH = 4
N_A_LAYERS = DEPTH // 2
RMS_EPS = 1e-6
LOG2E = 1.4426950408889634
QK_SCALE_LOG2 = HEAD_DIM ** -0.5 * LOG2E

LANES = 128
ROW_TILE = 688
LAST_ROW_TILE = 1024
PROJ_TN = 2048
NORM_ROWS = 16
NORM_UNROLL = 8
GATE_PARTS = 3
GATE_BLOCK = 688
SUB = 256
Q_TILE = 1024
ATT_LOOKAHEAD = 6
SB_TERMS_LAG = 6
SB_WEIGHTS_LAG = 3
SUM_ROWS = 16
NEG_BIG = -1e30
VMEM_LIMIT = 56 * 1024 * 1024

F32 = jnp.float32
BF16 = jnp.bfloat16


def _rms_norm_rows(x_ref, g_ref, a_ref):
    n_rows = x_ref.shape[0]
    g = g_ref[...]

    def body(i, _):
        r0 = pl.multiple_of(i * NORM_ROWS, NORM_ROWS)
        x = x_ref[pl.ds(r0, NORM_ROWS), :]
        ms = jnp.mean(x * x, axis=-1, keepdims=True)
        a_ref[pl.ds(r0, NORM_ROWS), :] = (x * lax.rsqrt(ms + RMS_EPS) * g).astype(a_ref.dtype)
        return 0

    lax.fori_loop(0, n_rows // NORM_ROWS, body, 0, unroll=NORM_UNROLL)


def _norm_matmul_kernel(x_ref, g_ref, w_ref, *rest, n_scaled_tiles, scale, with_side):
    if with_side:
        ws_ref, o_ref, side_ref, a_ref = rest
    else:
        o_ref, a_ref = rest
    n = pl.program_id(1)

    @pl.when(n == 0)
    def _():
        _rms_norm_rows(x_ref, g_ref, a_ref)
        if with_side:
            side_ref[...] = jnp.dot(a_ref[...], ws_ref[...], preferred_element_type=F32)

    acc = jnp.dot(a_ref[...], w_ref[...], preferred_element_type=F32)
    if n_scaled_tiles:
        acc = acc * jnp.where(n < n_scaled_tiles, jnp.float32(scale), jnp.float32(1.0))
    o_ref[...] = acc.astype(o_ref.dtype)


def _norm_matmul(h, g, w, *, n_out, out_dtype, tn, n_scaled_cols=0, scale=1.0, w_side=None,
                 w_layer=0, name):
    t, d = h.shape
    grid = (t // ROW_TILE, n_out // tn)
    with_side = w_side is not None
    if w.ndim == 3:
        w_spec = pl.BlockSpec((None, d, tn), lambda m, n: (w_layer, 0, n))
    else:
        w_spec = pl.BlockSpec((d, tn), lambda m, n: (0, n))
    in_specs = [
        pl.BlockSpec((ROW_TILE, d), lambda m, n: (m, 0)),
        pl.BlockSpec((1, d), lambda m, n: (0, 0)),
        w_spec,
    ]
    out_specs = pl.BlockSpec((ROW_TILE, tn), lambda m, n: (m, n))
    out_shape = jax.ShapeDtypeStruct((t, n_out), out_dtype)
    args = (h, g.reshape(1, d), w)
    if with_side:
        n_side = w_side.shape[1]
        in_specs.append(pl.BlockSpec((d, n_side), lambda m, n: (0, 0)))
        out_specs = (out_specs, pl.BlockSpec((ROW_TILE, n_side), lambda m, n: (m, 0)))
        out_shape = (out_shape, jax.ShapeDtypeStruct((t, n_side), F32))
        args += (w_side,)
    return pl.pallas_call(
        functools.partial(_norm_matmul_kernel, n_scaled_tiles=n_scaled_cols // tn, scale=scale,
                          with_side=with_side),
        grid=grid,
        in_specs=in_specs,
        out_specs=out_specs,
        out_shape=out_shape,
        scratch_shapes=[pltpu.VMEM((ROW_TILE, d), BF16)],
        compiler_params=pltpu.CompilerParams(
            dimension_semantics=("arbitrary", "arbitrary"), vmem_limit_bytes=VMEM_LIMIT),
        name=name,
    )(*args)


def _matmul_residual_kernel(o_ref, w_ref, h_ref, out_ref):
    out_ref[...] = h_ref[...] + jnp.dot(o_ref[...], w_ref[...], preferred_element_type=F32)


def _matmul_residual(o, w, layer, h, *, tn, name):
    t, d = o.shape
    n_out = w.shape[2]
    grid = (n_out // tn, t // ROW_TILE)
    return pl.pallas_call(
        _matmul_residual_kernel,
        grid=grid,
        in_specs=[
            pl.BlockSpec((ROW_TILE, d), lambda n, m: (m, 0)),
            pl.BlockSpec((None, d, tn), lambda n, m: (layer, 0, n)),
            pl.BlockSpec((ROW_TILE, tn), lambda n, m: (m, n)),
        ],
        out_specs=pl.BlockSpec((ROW_TILE, tn), lambda n, m: (m, n)),
        out_shape=jax.ShapeDtypeStruct((t, n_out), F32),
        compiler_params=pltpu.CompilerParams(
            dimension_semantics=("arbitrary", "arbitrary"), vmem_limit_bytes=VMEM_LIMIT),
        name=name,
    )(o, w, h)


def _mlp_kernel(x_ref, g_ref, wu_ref, wd_ref, gf_ref, o_ref, a_ref, *, final_norm):
    if len(x_ref.shape) == 3:
        x_ref = x_ref.at[0]
    f = pl.program_id(1)

    @pl.when(f == 0)
    def _():
        _rms_norm_rows(x_ref, g_ref, a_ref)
        o_ref[...] = x_ref[...]

    u = jnp.dot(a_ref[...], wu_ref[...], preferred_element_type=F32)
    u = jnp.square(jnp.maximum(u, 0.0)).astype(BF16)
    o_ref[...] += jnp.dot(u, wd_ref[...], preferred_element_type=F32)

    if final_norm:
        @pl.when(f == pl.num_programs(1) - 1)
        def _():
            gf = gf_ref[...]
            for r0 in range(0, o_ref.shape[0], NORM_ROWS):
                y = o_ref[r0:r0 + NORM_ROWS, :]
                ms = jnp.mean(y * y, axis=-1, keepdims=True)
                o_ref[r0:r0 + NORM_ROWS, :] = y * lax.rsqrt(ms + RMS_EPS) * gf


def _mlp(h, g, w_up, w_down, layer, g_final, *, tf, name):
    t, d = h.shape
    d_ff = w_up.shape[2]
    grid = (t // ROW_TILE, d_ff // tf)
    return pl.pallas_call(
        functools.partial(_mlp_kernel, final_norm=False),
        grid=grid,
        in_specs=[
            pl.BlockSpec((ROW_TILE, d), lambda m, f: (m, 0)),
            pl.BlockSpec((1, d), lambda m, f: (0, 0)),
            pl.BlockSpec((None, d, tf), lambda m, f: (layer, 0, f)),
            pl.BlockSpec((None, tf, d), lambda m, f: (layer, f, 0)),
            pl.BlockSpec((1, d), lambda m, f: (0, 0)),
        ],
        out_specs=pl.BlockSpec((ROW_TILE, d), lambda m, f: (m, 0)),
        out_shape=jax.ShapeDtypeStruct((t, d), F32),
        scratch_shapes=[pltpu.VMEM((ROW_TILE, d), BF16)],
        compiler_params=pltpu.CompilerParams(
            dimension_semantics=("arbitrary", "arbitrary"), vmem_limit_bytes=VMEM_LIMIT),
        name=name,
    )(h, g.reshape(1, d), w_up, w_down, g_final.reshape(1, d))


def _last_mlp(h, g, w_up, w_down, layer, g_final, *, batch, tf, name):
    t, d = h.shape
    length = t // batch
    seq = length - N_META
    d_ff = w_up.shape[2]
    tiles = seq // LAST_ROW_TILE
    grid = (batch * tiles, d_ff // tf)
    return pl.pallas_call(
        functools.partial(_mlp_kernel, final_norm=True),
        grid=grid,
        in_specs=[
            pl.BlockSpec((pl.Element(1), pl.Element(LAST_ROW_TILE), pl.Element(d)),
                         lambda m, f: (m // tiles, pl.multiple_of(
                             N_META + (m % tiles) * LAST_ROW_TILE, N_META), 0)),
            pl.BlockSpec((1, d), lambda m, f: (0, 0)),
            pl.BlockSpec((None, d, tf), lambda m, f: (layer, 0, f)),
            pl.BlockSpec((None, tf, d), lambda m, f: (layer, f, 0)),
            pl.BlockSpec((1, d), lambda m, f: (0, 0)),
        ],
        out_specs=pl.BlockSpec((None, LAST_ROW_TILE, d), lambda m, f: (m // tiles, m % tiles, 0)),
        out_shape=jax.ShapeDtypeStruct((batch, seq, d), F32),
        scratch_shapes=[pltpu.VMEM((LAST_ROW_TILE, d), BF16)],
        compiler_params=pltpu.CompilerParams(
            dimension_semantics=("arbitrary", "arbitrary"), vmem_limit_bytes=VMEM_LIMIT),
        name=name,
    )(h.reshape(batch, length, d), g.reshape(1, d), w_up, w_down, g_final.reshape(1, d))


def _split3_bf16(x):
    hi = x.astype(BF16)
    r = x - hi.astype(F32)
    mid = r.astype(BF16)
    lo = (r - mid.astype(F32)).astype(BF16)
    return hi, mid, lo


def _gate_selectors(n_heads):
    width = n_heads * LANES
    sel = np.zeros((LANES, 2 * width), np.float32)
    ones = np.zeros((1, 2 * width), np.float32)
    for h in range(n_heads):
        for p in range(GATE_PARTS):
            sel[p * n_heads + h, h * LANES + p] = 1.0
            sel[p * n_heads + h, width + h * LANES + GATE_PARTS + p] = -1.0
            ones[0, h * LANES + GATE_PARTS + p] = 1.0
            ones[0, width + h * LANES + p] = 1.0
    return jnp.asarray(sel, BF16), jnp.asarray(ones, F32)


def _gate_kernel(f_ref, b_ref, sel_ref, ones_ref, qx_ref, kx_ref, carry_ref):
    rows = f_ref.shape[1]
    width = qx_ref.shape[1]
    n_heads = width // LANES

    @pl.when(pl.program_id(1) == 0)
    def _():
        carry_ref[...] = jnp.zeros_like(carry_ref)

    x = f_ref[0] + b_ref[...]
    logf = jnp.minimum(x, 0.0) - jnp.log1p(jnp.exp(-jnp.abs(x)))
    ri = lax.broadcasted_iota(jnp.int32, (rows, rows), 0)
    ci = lax.broadcasted_iota(jnp.int32, (rows, rows), 1)
    tri = jnp.where(ci <= ri, 1.0, 0.0).astype(BF16)
    hi, mid, lo = _split3_bf16(logf)
    c = (jnp.dot(tri, hi, preferred_element_type=F32)
         + jnp.dot(tri, mid, preferred_element_type=F32)
         + jnp.dot(tri, lo, preferred_element_type=F32)) + carry_ref[...]
    carry_ref[...] = c[rows - 1:rows, :]
    hi, mid, lo = _split3_bf16(c * LOG2E)
    group = lax.broadcasted_iota(jnp.int32, hi.shape, 1) // n_heads
    parts = jnp.where(group == 0, hi, jnp.where(group == 1, mid, lo))
    out = jnp.dot(parts, sel_ref[...], preferred_element_type=F32) + ones_ref[...]
    qx_ref[...] = out[:, :width].astype(qx_ref.dtype)
    kx_ref[...] = out[:, width:].astype(kx_ref.dtype)


def _gate_bias_columns(f_logit, bias, n_heads):
    b, length, lanes = f_logit.shape
    n_blk = length // GATE_BLOCK
    width = n_heads * LANES
    sel, ones = _gate_selectors(n_heads)
    out = jax.ShapeDtypeStruct((b * length, width), BF16)
    out_spec = pl.BlockSpec((GATE_BLOCK, width), lambda i, r: (i * n_blk + r, 0))
    return pl.pallas_call(
        _gate_kernel,
        grid=(b, n_blk),
        in_specs=[
            pl.BlockSpec((1, GATE_BLOCK, lanes), lambda i, r: (i, r, 0)),
            pl.BlockSpec((1, lanes), lambda i, r: (0, 0)),
            pl.BlockSpec(sel.shape, lambda i, r: (0, 0)),
            pl.BlockSpec(ones.shape, lambda i, r: (0, 0)),
        ],
        out_specs=(out_spec, out_spec),
        out_shape=(out, out),
        scratch_shapes=[pltpu.VMEM((1, lanes), F32)],
        compiler_params=pltpu.CompilerParams(
            dimension_semantics=("arbitrary", "arbitrary"), vmem_limit_bytes=VMEM_LIMIT),
        name="gate_cumsum",
    )(f_logit, bias, sel, ones)


def _nt(a, b):
    return lax.dot_general(a, b, (((1,), (1,)), ((), ())), preferred_element_type=F32)


def _pad_rows(x, rows):
    return jnp.concatenate([x, jnp.zeros((rows - x.shape[0], x.shape[1]), x.dtype)], axis=0)


def _key_le_query(n):
    ki = lax.broadcasted_iota(jnp.int32, (n, n), 0)
    qi = lax.broadcasted_iota(jnp.int32, (n, n), 1)
    return ki <= qi


def _key_lt_query(n):
    ki = lax.broadcasted_iota(jnp.int32, (n, n), 0)
    qi = lax.broadcasted_iota(jnp.int32, (n, n), 1)
    return ki < qi


def _fox_kernel(q_ref, k_ref, v_ref, qx_ref, kx_ref, o_ref, vt_ref, qt_ref, acc_ref, m_ref):
    length = q_ref.shape[0]
    seq = length - N_META
    tq = min(Q_TILE, seq)
    n_sub = tq // SUB

    def q_rows(r0, rows):
        return jnp.concatenate([q_ref[r0:r0 + rows, :], qx_ref[r0:r0 + rows, :]], axis=1)

    def k_rows(r0, rows):
        return jnp.concatenate([k_ref[r0:r0 + rows, :], kx_ref[r0:r0 + rows, :]], axis=1)

    def with_ones_row(vt):
        ri = lax.broadcasted_iota(jnp.int32, (SUM_ROWS, vt.shape[1]), 0)
        return jnp.concatenate([vt, jnp.where(ri == 0, 1.0, 0.0).astype(vt.dtype)], axis=0)

    def normalised(acc):
        return acc[0:HEAD_DIM, :] * (1.0 / acc[HEAD_DIM:HEAD_DIM + 1, :])

    vt_ref[...] = with_ones_row(v_ref[N_META:, :].T)
    vt_head = with_ones_row(v_ref[0:LANES, :].T)
    qt_ref[0:HEAD_DIM, :] = q_ref[N_META:, :].T
    qt_ref[HEAD_DIM:, :] = qx_ref[N_META:, :].T

    s = _nt(k_rows(0, LANES), q_rows(0, LANES))
    s = jnp.where(_key_le_query(LANES), s, NEG_BIG)
    p = jnp.exp2(s - jnp.max(s, axis=0, keepdims=True))
    acc = jnp.dot(vt_head, p.astype(BF16), preferred_element_type=F32)
    o_ref[0:N_META, :] = normalised(acc).T[0:N_META, :].astype(o_ref.dtype)

    tasks = []
    for t0 in range(0, seq, tq):
        nb = t0 // SUB
        tasks += [(None, nb + u, False) for u in range(n_sub)]
        tasks += [(j, nb + u, False) for j in range(nb) for u in range(n_sub)]
        tasks += [(nb + jj, nb + u, u == jj) for jj in range(n_sub) for u in range(jj, n_sub)]

    def scores(task):
        j, c, _ = task
        k_blk = k_rows(0, N_META) if j is None else k_rows(N_META + j * SUB, SUB)
        return jnp.dot(k_blk, qt_ref[:, c * SUB:(c + 1) * SUB], preferred_element_type=F32)

    def consume(task, s, s_ahead):
        j, c, masked = task
        a0 = c * SUB
        if j is None:
            m = jnp.max(s, axis=0, keepdims=True)
            p = jnp.exp2(s - m)
            m_ref[:, a0:a0 + SUB] = m
            acc_ref[:, a0:a0 + SUB] = jnp.dot(vt_head, _pad_rows(p.astype(BF16), LANES),
                                              preferred_element_type=F32)
            return
        if masked:
            s = jnp.where(_key_le_query(SUB), s, NEG_BIG)
        m_old = m_ref[:, a0:a0 + SUB]
        m_new = jnp.maximum(m_old, jnp.max(s, axis=0, keepdims=True))
        if s_ahead is not None:
            m_new = m_new + _zero_row_after(s_ahead)
        alpha = jnp.exp2(m_old - m_new)
        p = jnp.exp2(s - m_new)
        pv = jnp.dot(vt_ref[:, j * SUB:(j + 1) * SUB], p.astype(BF16),
                     preferred_element_type=F32)
        acc = alpha * acc_ref[:, a0:a0 + SUB] + pv
        if masked:
            r = N_META + a0
            o_ref[r:r + SUB, :] = normalised(acc).T.astype(o_ref.dtype)
        else:
            m_ref[:, a0:a0 + SUB] = m_new
            acc_ref[:, a0:a0 + SUB] = acc

    pending = {}
    for i in range(len(tasks) + ATT_LOOKAHEAD):
        if i < len(tasks):
            pending[i] = scores(tasks[i])
        if i >= ATT_LOOKAHEAD:
            consume(tasks[i - ATT_LOOKAHEAD], pending.pop(i - ATT_LOOKAHEAD),
                    pending.get(i - ATT_LOOKAHEAD + 1))


def _fox_attention(qkv, qx, kx, batch):
    t = qkv.shape[0]
    n_heads = qkv.shape[1] // (3 * HEAD_DIM)
    length = t // batch
    seq = length - N_META
    tq = min(Q_TILE, seq)
    return pl.pallas_call(
        _fox_kernel,
        grid=(batch, n_heads),
        in_specs=[
            pl.BlockSpec((length, HEAD_DIM), lambda b, h: (b, h)),
            pl.BlockSpec((length, HEAD_DIM), lambda b, h: (b, n_heads + h)),
            pl.BlockSpec((length, HEAD_DIM), lambda b, h: (b, 2 * n_heads + h)),
            pl.BlockSpec((length, LANES), lambda b, h: (b, h)),
            pl.BlockSpec((length, LANES), lambda b, h: (b, h)),
        ],
        out_specs=pl.BlockSpec((length, HEAD_DIM), lambda b, h: (b, h)),
        out_shape=jax.ShapeDtypeStruct((t, n_heads * HEAD_DIM), BF16),
        scratch_shapes=[
            pltpu.VMEM((HEAD_DIM + SUM_ROWS, seq), BF16),
            pltpu.VMEM((HEAD_DIM + LANES, seq), BF16),
            pltpu.VMEM((HEAD_DIM + SUM_ROWS, seq), F32),
            pltpu.VMEM((1, seq), F32),
        ],
        compiler_params=pltpu.CompilerParams(
            dimension_semantics=("arbitrary", "arbitrary"), vmem_limit_bytes=VMEM_LIMIT),
        name="fox_attention",
    )(qkv, qkv, qkv, qx, kx)


def _suffix_matrix(rows, width):
    si = lax.broadcasted_iota(jnp.int32, (rows, width), 0)
    ji = lax.broadcasted_iota(jnp.int32, (rows, width), 1)
    return jnp.where((ji > si) & (ji < rows), 1.0, 0.0).astype(BF16)


def _zero_row_after(x):
    return jnp.minimum(jnp.maximum(x[0:1, :], 0.0), 0.0)


def _sb_kernel(q_ref, k_ref, v_ref, o_ref, vt_ref, qt_ref, acc_ref, run_ref, smat_ref):
    length = q_ref.shape[0]
    seq = length - N_META
    tq = min(Q_TILE, seq)
    n_sub = tq // SUB

    vt_ref[...] = v_ref[N_META:, :].T
    qt_ref[...] = q_ref[N_META:, :].T
    vt_head = v_ref[0:LANES, :].T
    k_meta = k_ref[0:N_META, :]
    smat_ref[...] = _suffix_matrix(SUB, SUB)
    smat_meta = _suffix_matrix(N_META, LANES)

    def terms(z, mask, suffix_mat, pad_to=None, z_ahead=None):
        one = 1.0 if z_ahead is None else 1.0 + _zero_row_after(z_ahead)
        pos = jnp.maximum(z, 0.0) + jnp.log2(one + jnp.exp2(-jnp.abs(z)))
        if mask is not None:
            pos = jnp.where(mask, pos, 0.0)
        hi = pos.astype(BF16)
        tot_first = hi[0:1, :].astype(F32)
        if pad_to is not None:
            hi = _pad_rows(hi, pad_to)
        suffix = jnp.dot(suffix_mat, hi, preferred_element_type=F32)
        return z - pos, suffix, suffix[0:1, :] + tot_first

    z = _nt(k_ref[0:LANES, :], q_ref[0:LANES, :])
    mask = _key_lt_query(LANES)
    w, suffix, _ = terms(z, mask, _suffix_matrix(LANES, LANES))
    a = jnp.where(mask, jnp.exp2(w - suffix), 0.0)
    acc = jnp.dot(vt_head, a.astype(BF16), preferred_element_type=F32)
    o_ref[0:N_META, :] = acc.T[0:N_META, :].astype(o_ref.dtype)

    for t0 in range(0, seq, tq):
        nb = t0 // SUB
        tasks = [(nb + jj, u, u == jj) for jj in reversed(range(n_sub)) for u in range(jj, n_sub)]
        tasks += [(j, u, False) for j in reversed(range(nb)) for u in range(n_sub)]
        tasks += [(None, u, False) for u in range(n_sub)]

        def stage_scores(task, t0=t0):
            j, u, _ = task
            k_blk = k_meta if j is None else k_ref[N_META + j * SUB:N_META + (j + 1) * SUB, :]
            c0 = t0 + u * SUB
            return jnp.dot(k_blk, qt_ref[:, c0:c0 + SUB], preferred_element_type=F32)

        def stage_terms(task, z, z_ahead):
            j, _, first = task
            if j is None:
                return terms(z, None, smat_meta, pad_to=LANES, z_ahead=z_ahead)
            return terms(z, _key_lt_query(SUB) if first else None, smat_ref[...], z_ahead=z_ahead)

        def stage_weights(task, w, suffix, tot, suffix_ahead, t0=t0):
            j, u, first = task
            a0 = u * SUB
            a = jnp.exp2(w - suffix)
            if first:
                a = jnp.where(_key_lt_query(SUB), a, 0.0)
            else:
                run = run_ref[:, a0:a0 + SUB]
                if suffix_ahead is not None:
                    run = run + _zero_row_after(suffix_ahead)
                decay = jnp.exp2(-run)
            if j is None:
                r = N_META + t0 + a0
                pv = jnp.dot(vt_head, _pad_rows(a.astype(BF16), LANES), preferred_element_type=F32)
                o_ref[r:r + SUB, :] = (acc_ref[:, a0:a0 + SUB] + pv * decay).T.astype(o_ref.dtype)
                return
            pv = jnp.dot(vt_ref[:, j * SUB:(j + 1) * SUB], a.astype(BF16),
                         preferred_element_type=F32)
            if first:
                acc_ref[:, a0:a0 + SUB] = pv
                run_ref[:, a0:a0 + SUB] = tot
            else:
                acc_ref[:, a0:a0 + SUB] += pv * decay
                run_ref[:, a0:a0 + SUB] += tot

        n = len(tasks)
        zs, ts = {}, {}
        for i in range(n + SB_TERMS_LAG + SB_WEIGHTS_LAG):
            b = i - SB_TERMS_LAG
            c = b - SB_WEIGHTS_LAG
            if i < n:
                zs[i] = stage_scores(tasks[i])
            if 0 <= b < n:
                ts[b] = stage_terms(tasks[b], zs.pop(b), zs.get(b + 1))
            if 0 <= c:
                ahead = ts.get(c + 1)
                stage_weights(tasks[c], *ts.pop(c), None if ahead is None else ahead[1])


def _sb_attention(q, kv, batch):
    t = q.shape[0]
    n_heads = q.shape[1] // HEAD_DIM
    length = t // batch
    seq = length - N_META
    tq = min(Q_TILE, seq)
    return pl.pallas_call(
        _sb_kernel,
        grid=(batch, n_heads),
        in_specs=[
            pl.BlockSpec((length, HEAD_DIM), lambda b, h: (b, h)),
            pl.BlockSpec((length, HEAD_DIM), lambda b, h: (b, h)),
            pl.BlockSpec((length, HEAD_DIM), lambda b, h: (b, n_heads + h)),
        ],
        out_specs=pl.BlockSpec((length, HEAD_DIM), lambda b, h: (b, h)),
        out_shape=jax.ShapeDtypeStruct((t, n_heads * HEAD_DIM), BF16),
        scratch_shapes=[
            pltpu.VMEM((HEAD_DIM, seq), BF16),
            pltpu.VMEM((HEAD_DIM, seq), BF16),
            pltpu.VMEM((HEAD_DIM, tq), F32),
            pltpu.VMEM((1, tq), F32),
            pltpu.VMEM((SUB, SUB), BF16),
        ],
        compiler_params=pltpu.CompilerParams(
            dimension_semantics=("arbitrary", "arbitrary"), vmem_limit_bytes=VMEM_LIMIT),
        name="sb_attention",
    )(q, kv, kv)


def kernel(x, meta_tokens, norm_attn, norm_mlp, w_up, w_down, fox_w_in, fox_b_f, fox_w_o,
           kv_norm, w_kv, sb_w_q, sb_w_o, final_norm):
    batch, seq, d = x.shape
    length = N_META + seq
    t = batch * length
    assert d == D_MODEL and t % ROW_TILE == 0 and seq % SUB == 0

    meta = jnp.broadcast_to(meta_tokens[None].astype(x.dtype), (batch, N_META, d))
    h = jnp.concatenate([meta, x], axis=1).reshape(t, d)

    fox_w_in_bf = fox_w_in.astype(BF16)
    fox_w_o_bf, sb_w_q_bf, sb_w_o_bf = fox_w_o.astype(BF16), sb_w_q.astype(BF16), sb_w_o.astype(BF16)
    w_up_bf, w_down_bf = w_up.astype(BF16), w_down.astype(BF16)

    kv = None
    for layer in range(DEPTH):
        if layer < N_A_LAYERS:
            i = layer
            pad = LANES - GATE_PARTS * N_HEADS
            w_f = jnp.pad(jnp.tile(fox_w_in_bf[i, :, 3 * d:], (1, GATE_PARTS)), ((0, 0), (0, pad)))
            b_f = jnp.pad(jnp.tile(fox_b_f[i].astype(F32), GATE_PARTS), (0, pad)).reshape(1, LANES)
            qkv, f_logit = _norm_matmul(h, norm_attn[layer], fox_w_in_bf, w_layer=i, n_out=3 * d,
                                        out_dtype=BF16, tn=PROJ_TN, n_scaled_cols=d,
                                        scale=QK_SCALE_LOG2, w_side=w_f, name="fox_qkv_proj")
            qx, kx = _gate_bias_columns(f_logit.reshape(batch, length, LANES), b_f, N_HEADS)
            o = _fox_attention(qkv, qx, kx, batch)
            h = _matmul_residual(o, fox_w_o_bf, i, h, tn=PROJ_TN, name="fox_out_proj")
        else:
            i = layer - N_A_LAYERS
            if kv is None:
                kv = _norm_matmul(h, kv_norm, w_kv.astype(BF16), n_out=2 * d, out_dtype=BF16,
                                  tn=PROJ_TN, name="shared_kv_proj")
            q = _norm_matmul(h, norm_attn[layer], sb_w_q_bf, w_layer=i, n_out=d, out_dtype=BF16,
                             tn=PROJ_TN, n_scaled_cols=d, scale=QK_SCALE_LOG2, name="sb_q_proj")
            o = _sb_attention(q, kv, batch)
            h = _matmul_residual(o, sb_w_o_bf, i, h, tn=PROJ_TN, name="sb_out_proj")
        if layer < DEPTH - 1:
            h = _mlp(h, norm_mlp[layer], w_up_bf, w_down_bf, layer, final_norm, tf=1024, name="mlp")
    return _last_mlp(h, norm_mlp[DEPTH - 1], w_up_bf, w_down_bf, DEPTH - 1, final_norm,
                     batch=batch, tf=512, name="last_mlp")
```

```python
import functools

import jax
import jax.numpy as jnp
import numpy as np
from jax import lax
from jax.experimental import pallas as pl
from jax.experimental.pallas import tpu as pltpu

D_MODEL = 2048
N_HEADS = 16
HEAD_DIM = D_MODEL // N_HEADS
D_FF = 4 * D_MODEL
N_META = 16
DEPTH = 4
N_A_LAYERS = DEPTH // 2
RMS_EPS = 1e-6
LOG2E = 1.4426950408889634
QK_SCALE_LOG2 = HEAD_DIM ** -0.5 * LOG2E

LANES = 128
ROW_TILE = 688
LAST_ROW_TILE = 1024
PROJ_TN = 2048
NORM_ROWS = 16
NORM_UNROLL = 8
GATE_PARTS = 3
GATE_BLOCK = 688
SUB = 256
Q_TILE = 1024
ATT_LOOKAHEAD = 6
SB_TERMS_LAG = 6
SB_WEIGHTS_LAG = 3
HEADS_PER_STEP = 2
SUM_ROWS = 16
NEG_BIG = -1e30
VMEM_LIMIT = 56 * 1024 * 1024

F32 = jnp.float32
BF16 = jnp.bfloat16


def _rms_norm_rows(x_ref, g_ref, a_ref):
    n_rows = x_ref.shape[0]
    g = g_ref[...]

    def body(i, _):
        r0 = pl.multiple_of(i * NORM_ROWS, NORM_ROWS)
        x = x_ref[pl.ds(r0, NORM_ROWS), :]
        ms = jnp.mean(x * x, axis=-1, keepdims=True)
        a_ref[pl.ds(r0, NORM_ROWS), :] = (x * lax.rsqrt(ms + RMS_EPS) * g).astype(a_ref.dtype)
        return 0

    lax.fori_loop(0, n_rows // NORM_ROWS, body, 0, unroll=NORM_UNROLL)


def _norm_matmul_kernel(x_ref, g_ref, w_ref, *rest, n_scaled_tiles, scale, with_side):
    if with_side:
        ws_ref, o_ref, side_ref, a_ref = rest
    else:
        o_ref, a_ref = rest
    n = pl.program_id(1)

    @pl.when(n == 0)
    def _():
        _rms_norm_rows(x_ref, g_ref, a_ref)
        if with_side:
            side_ref[...] = jnp.dot(a_ref[...], ws_ref[...], preferred_element_type=F32)

    acc = jnp.dot(a_ref[...], w_ref[...], preferred_element_type=F32)
    if n_scaled_tiles:
        acc = acc * jnp.where(n < n_scaled_tiles, jnp.float32(scale), jnp.float32(1.0))
    o_ref[...] = acc.astype(o_ref.dtype)


def _norm_matmul(h, g, w, *, n_out, out_dtype, tn, n_scaled_cols=0, scale=1.0, w_side=None,
                 w_layer=0, name):
    t, d = h.shape
    grid = (t // ROW_TILE, n_out // tn)
    with_side = w_side is not None
    if w.ndim == 3:
        w_spec = pl.BlockSpec((None, d, tn), lambda m, n: (w_layer, 0, n))
    else:
        w_spec = pl.BlockSpec((d, tn), lambda m, n: (0, n))
    in_specs = [
        pl.BlockSpec((ROW_TILE, d), lambda m, n: (m, 0)),
        pl.BlockSpec((1, d), lambda m, n: (0, 0)),
        w_spec,
    ]
    out_specs = pl.BlockSpec((ROW_TILE, tn), lambda m, n: (m, n))
    out_shape = jax.ShapeDtypeStruct((t, n_out), out_dtype)
    args = (h, g.reshape(1, d), w)
    if with_side:
        n_side = w_side.shape[1]
        in_specs.append(pl.BlockSpec((d, n_side), lambda m, n: (0, 0)))
        out_specs = (out_specs, pl.BlockSpec((ROW_TILE, n_side), lambda m, n: (m, 0)))
        out_shape = (out_shape, jax.ShapeDtypeStruct((t, n_side), F32))
        args += (w_side,)
    return pl.pallas_call(
        functools.partial(_norm_matmul_kernel, n_scaled_tiles=n_scaled_cols // tn, scale=scale,
                          with_side=with_side),
        grid=grid,
        in_specs=in_specs,
        out_specs=out_specs,
        out_shape=out_shape,
        scratch_shapes=[pltpu.VMEM((ROW_TILE, d), BF16)],
        compiler_params=pltpu.CompilerParams(
            dimension_semantics=("arbitrary", "arbitrary"), vmem_limit_bytes=VMEM_LIMIT),
        name=name,
    )(*args)


def _matmul_residual_kernel(o_ref, w_ref, h_ref, out_ref):
    out_ref[...] = h_ref[...] + jnp.dot(o_ref[...], w_ref[...], preferred_element_type=F32)


def _matmul_residual(o, w, layer, h, *, tn, name):
    t, d = o.shape
    n_out = w.shape[2]
    grid = (n_out // tn, t // ROW_TILE)
    return pl.pallas_call(
        _matmul_residual_kernel,
        grid=grid,
        in_specs=[
            pl.BlockSpec((ROW_TILE, d), lambda n, m: (m, 0)),
            pl.BlockSpec((None, d, tn), lambda n, m: (layer, 0, n)),
            pl.BlockSpec((ROW_TILE, tn), lambda n, m: (m, n)),
        ],
        out_specs=pl.BlockSpec((ROW_TILE, tn), lambda n, m: (m, n)),
        out_shape=jax.ShapeDtypeStruct((t, n_out), F32),
        compiler_params=pltpu.CompilerParams(
            dimension_semantics=("arbitrary", "arbitrary"), vmem_limit_bytes=VMEM_LIMIT),
        name=name,
    )(o, w, h)


def _mlp_kernel(x_ref, g_ref, wu_ref, wd_ref, gf_ref, o_ref, a_ref, *, final_norm):
    if len(x_ref.shape) == 3:
        x_ref = x_ref.at[0]
    f = pl.program_id(1)

    @pl.when(f == 0)
    def _():
        _rms_norm_rows(x_ref, g_ref, a_ref)
        o_ref[...] = x_ref[...]

    u = jnp.dot(a_ref[...], wu_ref[...], preferred_element_type=F32)
    u = jnp.square(jnp.maximum(u, 0.0)).astype(BF16)
    o_ref[...] += jnp.dot(u, wd_ref[...], preferred_element_type=F32)

    if final_norm:
        @pl.when(f == pl.num_programs(1) - 1)
        def _():
            gf = gf_ref[...]
            for r0 in range(0, o_ref.shape[0], NORM_ROWS):
                y = o_ref[r0:r0 + NORM_ROWS, :]
                ms = jnp.mean(y * y, axis=-1, keepdims=True)
                o_ref[r0:r0 + NORM_ROWS, :] = y * lax.rsqrt(ms + RMS_EPS) * gf


def _mlp(h, g, w_up, w_down, layer, g_final, *, tf, name):
    t, d = h.shape
    d_ff = w_up.shape[2]
    grid = (t // ROW_TILE, d_ff // tf)
    return pl.pallas_call(
        functools.partial(_mlp_kernel, final_norm=False),
        grid=grid,
        in_specs=[
            pl.BlockSpec((ROW_TILE, d), lambda m, f: (m, 0)),
            pl.BlockSpec((1, d), lambda m, f: (0, 0)),
            pl.BlockSpec((None, d, tf), lambda m, f: (layer, 0, f)),
            pl.BlockSpec((None, tf, d), lambda m, f: (layer, f, 0)),
            pl.BlockSpec((1, d), lambda m, f: (0, 0)),
        ],
        out_specs=pl.BlockSpec((ROW_TILE, d), lambda m, f: (m, 0)),
        out_shape=jax.ShapeDtypeStruct((t, d), F32),
        scratch_shapes=[pltpu.VMEM((ROW_TILE, d), BF16)],
        compiler_params=pltpu.CompilerParams(
            dimension_semantics=("arbitrary", "arbitrary"), vmem_limit_bytes=VMEM_LIMIT),
        name=name,
    )(h, g.reshape(1, d), w_up, w_down, g_final.reshape(1, d))


def _last_mlp(h, g, w_up, w_down, layer, g_final, *, batch, tf, name):
    t, d = h.shape
    length = t // batch
    seq = length - N_META
    d_ff = w_up.shape[2]
    tiles = seq // LAST_ROW_TILE
    grid = (batch * tiles, d_ff // tf)
    return pl.pallas_call(
        functools.partial(_mlp_kernel, final_norm=True),
        grid=grid,
        in_specs=[
            pl.BlockSpec((pl.Element(1), pl.Element(LAST_ROW_TILE), pl.Element(d)),
                         lambda m, f: (m // tiles, pl.multiple_of(
                             N_META + (m % tiles) * LAST_ROW_TILE, N_META), 0)),
            pl.BlockSpec((1, d), lambda m, f: (0, 0)),
            pl.BlockSpec((None, d, tf), lambda m, f: (layer, 0, f)),
            pl.BlockSpec((None, tf, d), lambda m, f: (layer, f, 0)),
            pl.BlockSpec((1, d), lambda m, f: (0, 0)),
        ],
        out_specs=pl.BlockSpec((None, LAST_ROW_TILE, d), lambda m, f: (m // tiles, m % tiles, 0)),
        out_shape=jax.ShapeDtypeStruct((batch, seq, d), F32),
        scratch_shapes=[pltpu.VMEM((LAST_ROW_TILE, d), BF16)],
        compiler_params=pltpu.CompilerParams(
            dimension_semantics=("arbitrary", "arbitrary"), vmem_limit_bytes=VMEM_LIMIT),
        name=name,
    )(h.reshape(batch, length, d), g.reshape(1, d), w_up, w_down, g_final.reshape(1, d))


def _split3_bf16(x):
    hi = x.astype(BF16)
    r = x - hi.astype(F32)
    mid = r.astype(BF16)
    lo = (r - mid.astype(F32)).astype(BF16)
    return hi, mid, lo


def _gate_selectors(n_heads):
    width = n_heads * LANES
    sel = np.zeros((LANES, 2 * width), np.float32)
    ones = np.zeros((1, 2 * width), np.float32)
    for h in range(n_heads):
        for p in range(GATE_PARTS):
            sel[p * n_heads + h, h * LANES + p] = 1.0
            sel[p * n_heads + h, width + h * LANES + GATE_PARTS + p] = -1.0
            ones[0, h * LANES + GATE_PARTS + p] = 1.0
            ones[0, width + h * LANES + p] = 1.0
    return jnp.asarray(sel, BF16), jnp.asarray(ones, F32)


def _gate_kernel(f_ref, b_ref, sel_ref, ones_ref, qx_ref, kx_ref, carry_ref):
    rows = f_ref.shape[1]
    width = qx_ref.shape[1]
    n_heads = width // LANES

    @pl.when(pl.program_id(1) == 0)
    def _():
        carry_ref[...] = jnp.zeros_like(carry_ref)

    x = f_ref[0] + b_ref[...]
    logf = jnp.minimum(x, 0.0) - jnp.log1p(jnp.exp(-jnp.abs(x)))
    ri = lax.broadcasted_iota(jnp.int32, (rows, rows), 0)
    ci = lax.broadcasted_iota(jnp.int32, (rows, rows), 1)
    tri = jnp.where(ci <= ri, 1.0, 0.0).astype(BF16)
    hi, mid, lo = _split3_bf16(logf)
    c = (jnp.dot(tri, hi, preferred_element_type=F32)
         + jnp.dot(tri, mid, preferred_element_type=F32)
         + jnp.dot(tri, lo, preferred_element_type=F32)) + carry_ref[...]
    carry_ref[...] = c[rows - 1:rows, :]
    hi, mid, lo = _split3_bf16(c * LOG2E)
    group = lax.broadcasted_iota(jnp.int32, hi.shape, 1) // n_heads
    parts = jnp.where(group == 0, hi, jnp.where(group == 1, mid, lo))
    out = jnp.dot(parts, sel_ref[...], preferred_element_type=F32) + ones_ref[...]
    qx_ref[...] = out[:, :width].astype(qx_ref.dtype)
    kx_ref[...] = out[:, width:].astype(kx_ref.dtype)


def _gate_bias_columns(f_logit, bias, n_heads):
    b, length, lanes = f_logit.shape
    n_blk = length // GATE_BLOCK
    width = n_heads * LANES
    sel, ones = _gate_selectors(n_heads)
    out = jax.ShapeDtypeStruct((b * length, width), BF16)
    out_spec = pl.BlockSpec((GATE_BLOCK, width), lambda i, r: (i * n_blk + r, 0))
    return pl.pallas_call(
        _gate_kernel,
        grid=(b, n_blk),
        in_specs=[
            pl.BlockSpec((1, GATE_BLOCK, lanes), lambda i, r: (i, r, 0)),
            pl.BlockSpec((1, lanes), lambda i, r: (0, 0)),
            pl.BlockSpec(sel.shape, lambda i, r: (0, 0)),
            pl.BlockSpec(ones.shape, lambda i, r: (0, 0)),
        ],
        out_specs=(out_spec, out_spec),
        out_shape=(out, out),
        scratch_shapes=[pltpu.VMEM((1, lanes), F32)],
        compiler_params=pltpu.CompilerParams(
            dimension_semantics=("arbitrary", "arbitrary"), vmem_limit_bytes=VMEM_LIMIT),
        name="gate_cumsum",
    )(f_logit, bias, sel, ones)


def _nt(a, b):
    return lax.dot_general(a, b, (((1,), (1,)), ((), ())), preferred_element_type=F32)


def _pad_rows(x, rows):
    return jnp.concatenate([x, jnp.zeros((rows - x.shape[0], x.shape[1]), x.dtype)], axis=0)


def _key_le_query(n):
    ki = lax.broadcasted_iota(jnp.int32, (n, n), 0)
    qi = lax.broadcasted_iota(jnp.int32, (n, n), 1)
    return ki <= qi


def _key_lt_query(n):
    ki = lax.broadcasted_iota(jnp.int32, (n, n), 0)
    qi = lax.broadcasted_iota(jnp.int32, (n, n), 1)
    return ki < qi


def _fox_kernel(q_ref, k_ref, v_ref, qx_ref, kx_ref, o_ref, *scratch):
    for hd in range(q_ref.shape[1] // HEAD_DIM):
        cols = pl.ds(hd * HEAD_DIM, HEAD_DIM)
        _fox_head(q_ref.at[:, cols], k_ref.at[:, cols], v_ref.at[:, cols], qx_ref.at[:, cols],
                  kx_ref.at[:, cols], o_ref.at[:, cols], *scratch)


def _fox_head(q_ref, k_ref, v_ref, qx_ref, kx_ref, o_ref, vt_ref, qt_ref, acc_ref, m_ref):
    length = q_ref.shape[0]
    seq = length - N_META
    tq = min(Q_TILE, seq)
    n_sub = tq // SUB

    def q_rows(r0, rows):
        return jnp.concatenate([q_ref[r0:r0 + rows, :], qx_ref[r0:r0 + rows, :]], axis=1)

    def k_rows(r0, rows):
        return jnp.concatenate([k_ref[r0:r0 + rows, :], kx_ref[r0:r0 + rows, :]], axis=1)

    def with_ones_row(vt):
        ri = lax.broadcasted_iota(jnp.int32, (SUM_ROWS, vt.shape[1]), 0)
        return jnp.concatenate([vt, jnp.where(ri == 0, 1.0, 0.0).astype(vt.dtype)], axis=0)

    def normalised(acc):
        return acc[0:HEAD_DIM, :] * (1.0 / acc[HEAD_DIM:HEAD_DIM + 1, :])

    vt_ref[...] = with_ones_row(v_ref[N_META:, :].T)
    vt_head = with_ones_row(v_ref[0:LANES, :].T)
    qt_ref[0:HEAD_DIM, :] = q_ref[N_META:, :].T
    qt_ref[HEAD_DIM:, :] = qx_ref[N_META:, :].T

    s = _nt(k_rows(0, LANES), q_rows(0, LANES))
    s = jnp.where(_key_le_query(LANES), s, NEG_BIG)
    p = jnp.exp2(s - jnp.max(s, axis=0, keepdims=True))
    acc = jnp.dot(vt_head, p.astype(BF16), preferred_element_type=F32)
    o_ref[0:N_META, :] = normalised(acc).T[0:N_META, :].astype(o_ref.dtype)

    tasks = []
    for t0 in range(0, seq, tq):
        nb = t0 // SUB
        tasks += [(None, nb + u, False) for u in range(n_sub)]
        tasks += [(j, nb + u, False) for j in range(nb) for u in range(n_sub)]
        tasks += [(nb + jj, nb + u, u == jj) for jj in range(n_sub) for u in range(jj, n_sub)]

    def scores(task):
        j, c, _ = task
        k_blk = k_rows(0, N_META) if j is None else k_rows(N_META + j * SUB, SUB)
        return jnp.dot(k_blk, qt_ref[:, c * SUB:(c + 1) * SUB], preferred_element_type=F32)

    def consume(task, s, s_ahead):
        j, c, masked = task
        a0 = c * SUB
        if j is None:
            m = jnp.max(s, axis=0, keepdims=True)
            p = jnp.exp2(s - m)
            m_ref[:, a0:a0 + SUB] = m
            acc_ref[:, a0:a0 + SUB] = jnp.dot(vt_head, _pad_rows(p.astype(BF16), LANES),
                                              preferred_element_type=F32)
            return
        if masked:
            s = jnp.where(_key_le_query(SUB), s, NEG_BIG)
        m_old = m_ref[:, a0:a0 + SUB]
        m_new = jnp.maximum(m_old, jnp.max(s, axis=0, keepdims=True))
        if s_ahead is not None:
            m_new = m_new + _zero_row_after(s_ahead)
        alpha = jnp.exp2(m_old - m_new)
        p = jnp.exp2(s - m_new)
        pv = jnp.dot(vt_ref[:, j * SUB:(j + 1) * SUB], p.astype(BF16),
                     preferred_element_type=F32)
        acc = alpha * acc_ref[:, a0:a0 + SUB] + pv
        if masked:
            r = N_META + a0
            o_ref[r:r + SUB, :] = normalised(acc).T.astype(o_ref.dtype)
        else:
            m_ref[:, a0:a0 + SUB] = m_new
            acc_ref[:, a0:a0 + SUB] = acc

    pending = {}
    for i in range(len(tasks) + ATT_LOOKAHEAD):
        if i < len(tasks):
            pending[i] = scores(tasks[i])
        if i >= ATT_LOOKAHEAD:
            consume(tasks[i - ATT_LOOKAHEAD], pending.pop(i - ATT_LOOKAHEAD),
                    pending.get(i - ATT_LOOKAHEAD + 1))


def _fox_attention(qkv, qx, kx, batch):
    t = qkv.shape[0]
    n_heads = qkv.shape[1] // (3 * HEAD_DIM)
    length = t // batch
    seq = length - N_META
    tq = min(Q_TILE, seq)
    return pl.pallas_call(
        _fox_kernel,
        grid=(batch, n_heads // HEADS_PER_STEP),
        in_specs=[
            pl.BlockSpec((length, HEADS_PER_STEP * HEAD_DIM), lambda b, h: (b, h)),
            pl.BlockSpec((length, HEADS_PER_STEP * HEAD_DIM),
                         lambda b, h: (b, n_heads // HEADS_PER_STEP + h)),
            pl.BlockSpec((length, HEADS_PER_STEP * HEAD_DIM),
                         lambda b, h: (b, 2 * n_heads // HEADS_PER_STEP + h)),
            pl.BlockSpec((length, HEADS_PER_STEP * LANES), lambda b, h: (b, h)),
            pl.BlockSpec((length, HEADS_PER_STEP * LANES), lambda b, h: (b, h)),
        ],
        out_specs=pl.BlockSpec((length, HEADS_PER_STEP * HEAD_DIM), lambda b, h: (b, h)),
        out_shape=jax.ShapeDtypeStruct((t, n_heads * HEAD_DIM), BF16),
        scratch_shapes=[
            pltpu.VMEM((HEAD_DIM + SUM_ROWS, seq), BF16),
            pltpu.VMEM((HEAD_DIM + LANES, seq), BF16),
            pltpu.VMEM((HEAD_DIM + SUM_ROWS, seq), F32),
            pltpu.VMEM((1, seq), F32),
        ],
        compiler_params=pltpu.CompilerParams(
            dimension_semantics=("arbitrary", "arbitrary"), vmem_limit_bytes=VMEM_LIMIT),
        name="fox_attention",
    )(qkv, qkv, qkv, qx, kx)


def _suffix_matrix(rows, width):
    si = lax.broadcasted_iota(jnp.int32, (rows, width), 0)
    ji = lax.broadcasted_iota(jnp.int32, (rows, width), 1)
    return jnp.where((ji > si) & (ji < rows), 1.0, 0.0).astype(BF16)


def _zero_row_after(x):
    return jnp.minimum(jnp.maximum(x[0:1, :], 0.0), 0.0)


def _sb_kernel(q_ref, k_ref, v_ref, o_ref, *scratch):
    for hd in range(q_ref.shape[1] // HEAD_DIM):
        cols = pl.ds(hd * HEAD_DIM, HEAD_DIM)
        _sb_head(q_ref.at[:, cols], k_ref.at[:, cols], v_ref.at[:, cols], o_ref.at[:, cols], *scratch)


def _sb_head(q_ref, k_ref, v_ref, o_ref, vt_ref, qt_ref, acc_ref, run_ref, smat_ref):
    length = q_ref.shape[0]
    seq = length - N_META
    tq = min(Q_TILE, seq)
    n_sub = tq // SUB

    vt_ref[...] = v_ref[N_META:, :].T
    qt_ref[...] = q_ref[N_META:, :].T
    vt_head = v_ref[0:LANES, :].T
    k_meta = k_ref[0:N_META, :]
    smat_ref[...] = _suffix_matrix(SUB, SUB)
    smat_meta = _suffix_matrix(N_META, LANES)

    def terms(z, mask, suffix_mat, pad_to=None, z_ahead=None):
        one = 1.0 if z_ahead is None else 1.0 + _zero_row_after(z_ahead)
        pos = jnp.maximum(z, 0.0) + jnp.log2(one + jnp.exp2(-jnp.abs(z)))
        if mask is not None:
            pos = jnp.where(mask, pos, 0.0)
        hi = pos.astype(BF16)
        tot_first = hi[0:1, :].astype(F32)
        if pad_to is not None:
            hi = _pad_rows(hi, pad_to)
        suffix = jnp.dot(suffix_mat, hi, preferred_element_type=F32)
        return z - pos, suffix, suffix[0:1, :] + tot_first

    z = _nt(k_ref[0:LANES, :], q_ref[0:LANES, :])
    mask = _key_lt_query(LANES)
    w, suffix, _ = terms(z, mask, _suffix_matrix(LANES, LANES))
    a = jnp.where(mask, jnp.exp2(w - suffix), 0.0)
    acc = jnp.dot(vt_head, a.astype(BF16), preferred_element_type=F32)
    o_ref[0:N_META, :] = acc.T[0:N_META, :].astype(o_ref.dtype)

    for t0 in range(0, seq, tq):
        nb = t0 // SUB
        tasks = [(nb + jj, u, u == jj) for jj in reversed(range(n_sub)) for u in range(jj, n_sub)]
        tasks += [(j, u, False) for j in reversed(range(nb)) for u in range(n_sub)]
        tasks += [(None, u, False) for u in range(n_sub)]

        def stage_scores(task, t0=t0):
            j, u, _ = task
            k_blk = k_meta if j is None else k_ref[N_META + j * SUB:N_META + (j + 1) * SUB, :]
            c0 = t0 + u * SUB
            return jnp.dot(k_blk, qt_ref[:, c0:c0 + SUB], preferred_element_type=F32)

        def stage_terms(task, z, z_ahead):
            j, _, first = task
            if j is None:
                return terms(z, None, smat_meta, pad_to=LANES, z_ahead=z_ahead)
            return terms(z, _key_lt_query(SUB) if first else None, smat_ref[...], z_ahead=z_ahead)

        def stage_weights(task, w, suffix, tot, suffix_ahead, t0=t0):
            j, u, first = task
            a0 = u * SUB
            a = jnp.exp2(w - suffix)
            if first:
                a = jnp.where(_key_lt_query(SUB), a, 0.0)
            else:
                run = run_ref[:, a0:a0 + SUB]
                if suffix_ahead is not None:
                    run = run + _zero_row_after(suffix_ahead)
                decay = jnp.exp2(-run)
            if j is None:
                r = N_META + t0 + a0
                pv = jnp.dot(vt_head, _pad_rows(a.astype(BF16), LANES), preferred_element_type=F32)
                o_ref[r:r + SUB, :] = (acc_ref[:, a0:a0 + SUB] + pv * decay).T.astype(o_ref.dtype)
                return
            pv = jnp.dot(vt_ref[:, j * SUB:(j + 1) * SUB], a.astype(BF16),
                         preferred_element_type=F32)
            if first:
                acc_ref[:, a0:a0 + SUB] = pv
                run_ref[:, a0:a0 + SUB] = tot
            else:
                acc_ref[:, a0:a0 + SUB] += pv * decay
                run_ref[:, a0:a0 + SUB] += tot

        n = len(tasks)
        zs, ts = {}, {}
        for i in range(n + SB_TERMS_LAG + SB_WEIGHTS_LAG):
            b = i - SB_TERMS_LAG
            c = b - SB_WEIGHTS_LAG
            if i < n:
                zs[i] = stage_scores(tasks[i])
            if 0 <= b < n:
                ts[b] = stage_terms(tasks[b], zs.pop(b), zs.get(b + 1))
            if 0 <= c:
                ahead = ts.get(c + 1)
                stage_weights(tasks[c], *ts.pop(c), None if ahead is None else ahead[1])


def _sb_attention(q, kv, batch):
    t = q.shape[0]
    n_heads = q.shape[1] // HEAD_DIM
    length = t // batch
    seq = length - N_META
    tq = min(Q_TILE, seq)
    return pl.pallas_call(
        _sb_kernel,
        grid=(batch, n_heads // HEADS_PER_STEP),
        in_specs=[
            pl.BlockSpec((length, HEADS_PER_STEP * HEAD_DIM), lambda b, h: (b, h)),
            pl.BlockSpec((length, HEADS_PER_STEP * HEAD_DIM), lambda b, h: (b, h)),
            pl.BlockSpec((length, HEADS_PER_STEP * HEAD_DIM),
                         lambda b, h: (b, n_heads // HEADS_PER_STEP + h)),
        ],
        out_specs=pl.BlockSpec((length, HEADS_PER_STEP * HEAD_DIM), lambda b, h: (b, h)),
        out_shape=jax.ShapeDtypeStruct((t, n_heads * HEAD_DIM), BF16),
        scratch_shapes=[
            pltpu.VMEM((HEAD_DIM, seq), BF16),
            pltpu.VMEM((HEAD_DIM, seq), BF16),
            pltpu.VMEM((HEAD_DIM, tq), F32),
            pltpu.VMEM((1, tq), F32),
            pltpu.VMEM((SUB, SUB), BF16),
        ],
        compiler_params=pltpu.CompilerParams(
            dimension_semantics=("arbitrary", "arbitrary"), vmem_limit_bytes=VMEM_LIMIT),
        name="sb_attention",
    )(q, kv, kv)


def kernel(x, meta_tokens, norm_attn, norm_mlp, w_up, w_down, fox_w_in, fox_b_f, fox_w_o,
           kv_norm, w_kv, sb_w_q, sb_w_o, final_norm):
    batch, seq, d = x.shape
    length = N_META + seq
    t = batch * length
    assert d == D_MODEL and t % ROW_TILE == 0 and seq % SUB == 0

    meta = jnp.broadcast_to(meta_tokens[None].astype(x.dtype), (batch, N_META, d))
    h = jnp.concatenate([meta, x], axis=1).reshape(t, d)

    fox_w_in_bf = fox_w_in.astype(BF16)
    fox_w_o_bf, sb_w_q_bf, sb_w_o_bf = fox_w_o.astype(BF16), sb_w_q.astype(BF16), sb_w_o.astype(BF16)
    w_up_bf, w_down_bf = w_up.astype(BF16), w_down.astype(BF16)

    kv = None
    for layer in range(DEPTH):
        if layer < N_A_LAYERS:
            i = layer
            pad = LANES - GATE_PARTS * N_HEADS
            w_f = jnp.pad(jnp.tile(fox_w_in_bf[i, :, 3 * d:], (1, GATE_PARTS)), ((0, 0), (0, pad)))
            b_f = jnp.pad(jnp.tile(fox_b_f[i].astype(F32), GATE_PARTS), (0, pad)).reshape(1, LANES)
            qkv, f_logit = _norm_matmul(h, norm_attn[layer], fox_w_in_bf, w_layer=i, n_out=3 * d,
                                        out_dtype=BF16, tn=PROJ_TN, n_scaled_cols=d,
                                        scale=QK_SCALE_LOG2, w_side=w_f, name="fox_qkv_proj")
            qx, kx = _gate_bias_columns(f_logit.reshape(batch, length, LANES), b_f, N_HEADS)
            o = _fox_attention(qkv, qx, kx, batch)
            h = _matmul_residual(o, fox_w_o_bf, i, h, tn=PROJ_TN, name="fox_out_proj")
        else:
            i = layer - N_A_LAYERS
            if kv is None:
                kv = _norm_matmul(h, kv_norm, w_kv.astype(BF16), n_out=2 * d, out_dtype=BF16,
                                  tn=PROJ_TN, name="shared_kv_proj")
            q = _norm_matmul(h, norm_attn[layer], sb_w_q_bf, w_layer=i, n_out=d, out_dtype=BF16,
                             tn=PROJ_TN, n_scaled_cols=d, scale=QK_SCALE_LOG2, name="sb_q_proj")
            o = _sb_attention(q, kv, batch)
            h = _matmul_residual(o, sb_w_o_bf, i, h, tn=PROJ_TN, name="sb_out_proj")
        if layer < DEPTH - 1:
            h = _mlp(h, norm_mlp[layer], w_up_bf, w_down_bf, layer, final_norm, tf=1024, name="mlp")
    return _last_mlp(h, norm_mlp[DEPTH - 1], w_up_bf, w_down_bf, DEPTH - 1, final_norm,
                     batch=batch, tf=512, name="last_mlp")
```
